```python
import math
import jax, jax.numpy as jnp
from jax import lax
import numpy as np

D_MODEL = 2048
BATCH = 2
SEQ = 4096
DEPTH = 1

D_MLSTM = 2048
MLSTM_HEADS = 4
MLSTM_HEAD_DIM = D_MLSTM // MLSTM_HEADS
QKV_BLOCK = 4
MLSTM_CHUNK = 128
D_SSD = 2048
SSD_HEAD_DIM = 64
SSD_HEADS = D_SSD // SSD_HEAD_DIM
SSD_GROUPS = 4
SSD_STATE = 128
SSD_CHUNK = 128
CONV_WIDTH = 4
D_MIX = D_MLSTM + D_SSD
IN_COLS = 2 * D_MLSTM + 2 * D_SSD + 2 * SSD_GROUPS * SSD_STATE + SSD_HEADS
MOE_GROUPS = 8
EXPERTS_PER_GROUP = 8
N_EXPERTS = MOE_GROUPS * EXPERTS_PER_GROUP
TOP_K = 2
D_FF_EXPERT = 1024
MOE_BLOCK = 128
DEEPNORM_ALPHA = (2 * DEPTH) ** 0.25
DEEPNORM_BETA = (8 * DEPTH) ** -0.25
LN_EPS = 1e-5

kernel_name = "hybrid_mlstm_ssd_hiermoe_deepnorm"


def layer_norm(x, g, b):
    xf = x.astype(jnp.float32)
    mu = xf.mean(-1, keepdims=True)
    var = jnp.square(xf - mu).mean(-1, keepdims=True)
    return ((xf - mu) * lax.rsqrt(var + LN_EPS) * g + b).astype(x.dtype)


def causal_depthwise_conv(x, w, b):
    c = x.shape[-1]
    y = lax.conv_general_dilated(x, w.astype(x.dtype)[:, None, :], window_strides=(1,),
                                 padding=[(CONV_WIDTH - 1, 0)],
                                 dimension_numbers=('NWC', 'WIO', 'NWC'),
                                 feature_group_count=c)
    return y + b.astype(x.dtype)


def segsum(a):
    t = a.shape[-1]
    ar = jnp.broadcast_to(a[..., :, None], a.shape + (t,))
    strict = jnp.tril(jnp.ones((t, t), dtype=bool), -1)
    cs = jnp.cumsum(jnp.where(strict, ar, 0.0), axis=-2)
    incl = jnp.tril(jnp.ones((t, t), dtype=bool))
    return jnp.where(incl, cs, -jnp.inf)


def headwise_linear(u, w):
    ub = u.reshape(u.shape[:-1] + (u.shape[-1] // QKV_BLOCK, QKV_BLOCK))
    return jnp.einsum('bsnd,nde->bsne', ub, w.astype(u.dtype)).reshape(u.shape)


def mlstm_mixer(xm, og, conv_w, conv_b, w_q, w_k, w_v, w_if, b_if, norm_w, skip):
    f32 = jnp.float32
    bsz, s = xm.shape[0], xm.shape[1]
    H, Dh, L = MLSTM_HEADS, MLSTM_HEAD_DIM, MLSTM_CHUNK
    nc = s // L
    xc = jax.nn.silu(causal_depthwise_conv(xm, conv_w, conv_b))
    q = headwise_linear(xc, w_q)
    k = headwise_linear(xc, w_k)
    v = headwise_linear(xm, w_v)
    gates = (jnp.concatenate([q, k, v], -1) @ w_if.astype(q.dtype)).astype(f32) + b_if
    ig = gates[..., :H]
    logf = jax.nn.log_sigmoid(gates[..., H:])

    def to_chunks(u):
        return u.astype(f32).reshape(bsz, nc, L, H, Dh).transpose(1, 0, 3, 2, 4)

    qc = to_chunks(q)
    kc = to_chunks(k) * (1.0 / math.sqrt(Dh))
    vc = to_chunks(v)
    igc = ig.reshape(bsz, nc, L, H).transpose(1, 0, 3, 2)
    lfc = logf.reshape(bsz, nc, L, H).transpose(1, 0, 3, 2)
    causal = jnp.tril(jnp.ones((L, L), dtype=bool))

    def step(carry, inp):
        cmat, nvec, m = carry
        qb, kb, vb, ib, fb = inp
        bcum = jnp.cumsum(fb, -1)
        dlog = jnp.where(causal, bcum[..., :, None] - bcum[..., None, :] + ib[..., None, :], -jnp.inf)
        inter = bcum + m[..., None]
        mt = jnp.maximum(inter, dlog.max(-1))
        w_intra = jnp.exp(dlog - mt[..., None])
        w_inter = jnp.exp(inter - mt)
        sc = jnp.einsum('bhtd,bhsd->bhts', qb, kb) * w_intra
        num = jnp.einsum('bhts,bhsd->bhtd', sc, vb) + w_inter[..., None] * jnp.einsum('bhvk,bhtk->bhtv', cmat, qb)
        den = sc.sum(-1) + w_inter * jnp.einsum('bhk,bhtk->bht', nvec, qb)
        hb = num / jnp.maximum(jnp.abs(den), jnp.exp(-mt))[..., None]
        bl = bcum[..., -1]
        wlog = bl[..., None] - bcum + ib
        m_new = jnp.maximum(bl + m, wlog.max(-1))
        ws = jnp.exp(wlog - m_new[..., None])
        decay = jnp.exp(bl + m - m_new)
        c_new = decay[..., None, None] * cmat + jnp.einsum('bhs,bhsv,bhsk->bhvk', ws, vb, kb)
        n_new = decay[..., None] * nvec + jnp.einsum('bhs,bhsk->bhk', ws, kb)
        return (c_new, n_new, m_new), hb

    init = (jnp.zeros((bsz, H, Dh, Dh), f32), jnp.zeros((bsz, H, Dh), f32), jnp.zeros((bsz, H), f32))
    _, hc = lax.scan(step, init, (qc, kc, vc, igc, lfc))
    h = hc.transpose(1, 0, 3, 2, 4).reshape(bsz, s, H, Dh)
    mu = h.mean(-1, keepdims=True)
    var = jnp.square(h - mu).mean(-1, keepdims=True)
    hn = ((h - mu) * lax.rsqrt(var + LN_EPS)).reshape(bsz, s, D_MLSTM) * norm_w
    hn = hn + skip * xc.astype(f32)
    return jax.nn.sigmoid(og.astype(f32)) * hn


def ssd_mixer(z, xs, bm, cm, dt_raw, conv_w, conv_b, dt_bias, a_log, d_skip, norm_w):
    f32 = jnp.float32
    bsz, s = xs.shape[0], xs.shape[1]
    G, N, P, L = SSD_GROUPS, SSD_STATE, SSD_HEAD_DIM, SSD_CHUNK
    E = SSD_HEADS // G
    GN = G * N
    nc = s // L
    xbc = jnp.concatenate([xs, bm, cm], -1)
    xbc = jax.nn.silu(causal_depthwise_conv(xbc, conv_w, conv_b)).astype(f32)
    xh = xbc[..., :D_SSD].reshape(bsz, s, SSD_HEADS, P)
    bc = xbc[..., D_SSD:D_SSD + GN].reshape(bsz, nc, L, G, N)
    cc = xbc[..., D_SSD + GN:].reshape(bsz, nc, L, G, N)
    dt = jax.nn.softplus(dt_raw.astype(f32) + dt_bias)
    a = -jnp.exp(a_log.astype(f32))
    xd = (xh * dt[..., None]).reshape(bsz, nc, L, G, E, P)
    adt = (dt * a).reshape(bsz, nc, L, G, E).transpose(0, 3, 4, 1, 2)
    a_cs = jnp.cumsum(adt, -1)
    cb = jnp.einsum('bclgn,bcsgn->bgcls', cc, bc)
    scores = cb[:, :, None] * jnp.exp(segsum(adt))
    y_diag = jnp.einsum('bgecls,bcsgep->bclgep', scores, xd)
    decay_states = jnp.exp(a_cs[..., -1:] - a_cs)
    states = jnp.einsum('bclgn,bgecl,bclgep->bcgepn', bc, decay_states, xd)
    states = jnp.concatenate([jnp.zeros_like(states[:, :1]), states], axis=1)
    chunk_decay = jnp.exp(segsum(jnp.pad(a_cs[..., -1], ((0, 0), (0, 0), (0, 0), (1, 0)))))
    states = jnp.einsum('bgezc,bcgepn->bzgepn', chunk_decay, states)[:, :-1]
    y_off = jnp.einsum('bclgn,bcgepn,bgecl->bclgep', cc, states, jnp.exp(a_cs))
    y = (y_diag + y_off).reshape(bsz, s, SSD_HEADS, P) + d_skip.astype(f32)[:, None] * xh
    y = y.reshape(bsz, s, D_SSD) * jax.nn.silu(z.astype(f32))
    yg = y.reshape(bsz, s, G, D_SSD // G)
    yg = yg * lax.rsqrt(jnp.square(yg).mean(-1, keepdims=True) + LN_EPS)
    return yg.reshape(bsz, s, D_SSD) * norm_w


def hier_moe(h, w_rc, b_rc, w_rf, b_rf, w_gate, w_up, w_down):
    f32 = jnp.float32
    bsz, s, d = h.shape
    T = bsz * s
    TK = T * TOP_K
    NB = TK // MOE_BLOCK + N_EXPERTS + 1
    NR = NB * MOE_BLOCK
    hf = h.reshape(T, d)
    pc = jax.nn.softmax((hf @ w_rc.astype(h.dtype)).astype(f32) + b_rc, axis=-1)
    pg, gidx = lax.top_k(pc, 1)
    lf = ((hf @ w_rf.astype(h.dtype)).astype(f32) + b_rf).reshape(T, MOE_GROUPS, EXPERTS_PER_GROUP)
    sel = jnp.take_along_axis(lf, jnp.broadcast_to(gidx[:, :, None], (T, 1, EXPERTS_PER_GROUP)), axis=1)[:, 0]
    pf = jax.nn.softmax(sel, axis=-1)
    v2, i2 = lax.top_k(pf, TOP_K)
    wts = v2 / v2.sum(-1, keepdims=True) * pg
    eid = gidx * EXPERTS_PER_GROUP + i2
    e_flat = eid.reshape(TK)
    w_flat = wts.reshape(TK)
    tok_flat = jnp.repeat(jnp.arange(T, dtype=jnp.int32), TOP_K)
    counts = jnp.zeros((N_EXPERTS,), jnp.int32).at[e_flat].add(1)
    padded = (counts + MOE_BLOCK - 1) // MOE_BLOCK * MOE_BLOCK
    pad_end = jnp.cumsum(padded)
    pad_start = pad_end - padded
    start = jnp.cumsum(counts) - counts
    order = jnp.argsort(e_flat)
    e_sorted = e_flat[order]
    dest = pad_start[e_sorted] + jnp.arange(TK, dtype=jnp.int32) - start[e_sorted]
    row_tok = jnp.full((NR,), T, jnp.int32).at[dest].set(tok_flat[order])
    row_w = jnp.zeros((NR,), f32).at[dest].set(w_flat[order])
    block_start = jnp.arange(NB, dtype=jnp.int32) * MOE_BLOCK
    block_e = jnp.minimum(jnp.searchsorted(pad_end, block_start, side='right'), N_EXPERTS - 1)
    x_pad = jnp.concatenate([hf, jnp.zeros((1, d), h.dtype)], 0)
    xin = x_pad[row_tok].reshape(NB, MOE_BLOCK, d)

    def expert_block(args):
        xb, e = args
        a = xb @ w_gate[e].astype(xb.dtype)
        u = xb @ w_up[e].astype(xb.dtype)
        return (jax.nn.silu(a) * u) @ w_down[e].astype(xb.dtype)

    yb = lax.map(expert_block, (xin, block_e)).reshape(NR, d)
    y = jnp.zeros((T + 1, d), f32).at[row_tok].add(yb.astype(f32) * row_w[:, None])[:T]
    return y.reshape(bsz, s, d).astype(h.dtype)


def setup_inputs(seed: int = 0) -> dict:
    key = jax.random.key(seed)
    ks = jax.random.split(key, 40)
    f32 = jnp.float32
    Ld = DEPTH
    beta = DEEPNORM_BETA

    def nrm(k, shape, scale):
        return jax.random.normal(k, shape, f32) * scale

    conv_ch_ssd = D_SSD + 2 * SSD_GROUPS * SSD_STATE
    col_scale = jnp.concatenate([
        jnp.ones((2 * D_MLSTM + D_SSD,), f32),
        jnp.full((D_SSD,), beta, f32),
        jnp.ones((2 * SSD_GROUPS * SSD_STATE + SSD_HEADS,), f32)])
    dt0 = jnp.exp(jax.random.uniform(ks[16], (Ld, SSD_HEADS), f32, math.log(1e-3), math.log(1e-1)))
    dt_bias = dt0 + jnp.log(-jnp.expm1(-dt0))
    b_if = jnp.concatenate([
        nrm(ks[11], (Ld, MLSTM_HEADS), 0.1),
        jnp.linspace(3.0, 6.0, MLSTM_HEADS, dtype=f32)[None, :] + nrm(ks[12], (Ld, MLSTM_HEADS), 0.1)], -1)
    nb = D_MLSTM // QKV_BLOCK
    return {
        "x": nrm(ks[0], (BATCH, SEQ, D_MODEL), 1.0),
        "ln_in_g": 1.0 + nrm(ks[1], (D_MODEL,), 0.02),
        "ln_in_b": nrm(ks[2], (D_MODEL,), 0.02),
        "w_in": nrm(ks[3], (Ld, D_MODEL, IN_COLS), D_MODEL ** -0.5) * col_scale,
        "conv_m_w": nrm(ks[4], (Ld, CONV_WIDTH, D_MLSTM), CONV_WIDTH ** -0.5),
        "conv_m_b": nrm(ks[5], (Ld, D_MLSTM), 0.02),
        "w_q": nrm(ks[6], (Ld, nb, QKV_BLOCK, QKV_BLOCK), QKV_BLOCK ** -0.5),
        "w_k": nrm(ks[7], (Ld, nb, QKV_BLOCK, QKV_BLOCK), QKV_BLOCK ** -0.5),
        "w_v": nrm(ks[8], (Ld, nb, QKV_BLOCK, QKV_BLOCK), QKV_BLOCK ** -0.5) * beta,
        "w_if": nrm(ks[9], (Ld, 3 * D_MLSTM, 2 * MLSTM_HEADS), 0.1 * (3 * D_MLSTM) ** -0.5),
        "b_if": b_if,
        "mlstm_norm_w": 1.0 + nrm(ks[13], (Ld, D_MLSTM), 0.02),
        "mlstm_skip": 1.0 + nrm(ks[14], (Ld, D_MLSTM), 0.02),
        "conv_s_w": nrm(ks[15], (Ld, CONV_WIDTH, conv_ch_ssd), CONV_WIDTH ** -0.5),
        "conv_s_b": nrm(ks[17], (Ld, conv_ch_ssd), 0.02),
        "dt_bias": dt_bias,
        "a_log": jnp.log(jax.random.uniform(ks[18], (Ld, SSD_HEADS), f32, 1.0, 16.0)),
        "d_skip": 1.0 + nrm(ks[19], (Ld, SSD_HEADS), 0.02),
        "ssd_norm_w": 1.0 + nrm(ks[20], (Ld, D_SSD), 0.02),
        "w_out": nrm(ks[21], (Ld, D_MIX, D_MODEL), D_MIX ** -0.5) * beta,
        "ln1_g": 1.0 + nrm(ks[22], (Ld, D_MODEL), 0.02),
        "ln1_b": nrm(ks[23], (Ld, D_MODEL), 0.02),
        "w_router_coarse": nrm(ks[24], (Ld, D_MODEL, MOE_GROUPS), D_MODEL ** -0.5),
        "b_router_coarse": nrm(ks[25], (Ld, MOE_GROUPS), 0.01),
        "w_router_fine": nrm(ks[26], (Ld, D_MODEL, N_EXPERTS), D_MODEL ** -0.5),
        "b_router_fine": nrm(ks[27], (Ld, N_EXPERTS), 0.01),
        "w_gate_e": nrm(ks[28], (Ld, N_EXPERTS, D_MODEL, D_FF_EXPERT), D_MODEL ** -0.5) * beta,
        "w_up_e": nrm(ks[29], (Ld, N_EXPERTS, D_MODEL, D_FF_EXPERT), D_MODEL ** -0.5) * beta,
        "w_down_e": nrm(ks[30], (Ld, N_EXPERTS, D_FF_EXPERT, D_MODEL), D_FF_EXPERT ** -0.5) * beta,
        "ln2_g": 1.0 + nrm(ks[31], (Ld, D_MODEL), 0.02),
        "ln2_b": nrm(ks[32], (Ld, D_MODEL), 0.02),
    }


def reference(x, ln_in_g, ln_in_b, w_in, conv_m_w, conv_m_b, w_q, w_k, w_v, w_if, b_if,
              mlstm_norm_w, mlstm_skip, conv_s_w, conv_s_b, dt_bias, a_log, d_skip, ssd_norm_w,
              w_out, ln1_g, ln1_b, w_router_coarse, b_router_coarse, w_router_fine, b_router_fine,
              w_gate_e, w_up_e, w_down_e, ln2_g, ln2_b):
    GN = SSD_GROUPS * SSD_STATE
    c0 = D_MLSTM
    c1 = c0 + D_MLSTM
    c2 = c1 + D_SSD
    c3 = c2 + D_SSD
    c4 = c3 + GN
    c5 = c4 + GN
    h = layer_norm(x, ln_in_g, ln_in_b)
    for l in range(DEPTH):
        proj = h @ w_in[l].astype(h.dtype)
        y_m = mlstm_mixer(proj[..., :c0], proj[..., c0:c1], conv_m_w[l], conv_m_b[l],
                          w_q[l], w_k[l], w_v[l], w_if[l], b_if[l], mlstm_norm_w[l], mlstm_skip[l])
        y_s = ssd_mixer(proj[..., c1:c2], proj[..., c2:c3], proj[..., c3:c4], proj[..., c4:c5],
                        proj[..., c5:], conv_s_w[l], conv_s_b[l], dt_bias[l], a_log[l], d_skip[l],
                        ssd_norm_w[l])
        mix = jnp.concatenate([y_m, y_s], -1).astype(h.dtype) @ w_out[l].astype(h.dtype)
        h = layer_norm(DEEPNORM_ALPHA * h + mix, ln1_g[l], ln1_b[l])
        moe = hier_moe(h, w_router_coarse[l], b_router_coarse[l], w_router_fine[l], b_router_fine[l],
                       w_gate_e[l], w_up_e[l], w_down_e[l])
        h = layer_norm(DEEPNORM_ALPHA * h + moe, ln2_g[l], ln2_b[l])
    return h
```

```python
import functools
import math

import jax
import jax.numpy as jnp
from jax import lax
from jax.experimental import pallas as pl
from jax.experimental.pallas import tpu as pltpu

F32 = jnp.float32
BF16 = jnp.bfloat16

LANES = 128
MXU_DIM = 256
VMEM_LIMIT = 56 * 1024 * 1024

MLSTM_HEADS = 4
QKV_BLOCK = 4
CHUNK = 128
SSD_HEAD_DIM = 64
SSD_GROUPS = 4
SSD_STATE = 128
CONV_WIDTH = 4
MOE_GROUPS = 8
EXPERTS_PER_GROUP = 8
TOP_K = 2
MOE_BLOCK = 128
DEPTH = 1
DEEPNORM_ALPHA = (2 * DEPTH) ** 0.25
LN_EPS = 1e-5
HALO = 8


def _params(semantics):
    return pltpu.CompilerParams(dimension_semantics=semantics, vmem_limit_bytes=VMEM_LIMIT)


def _bdot(a, b):
    return jnp.dot(a.astype(BF16), b.astype(BF16), preferred_element_type=F32)


def _bdot_nt(a, b):
    return lax.dot_general(a.astype(BF16), b.astype(BF16), (((1,), (1,)), ((), ())),
                           preferred_element_type=F32)


def _split2(a):
    hi = a.astype(BF16)
    lo = (a - hi.astype(F32)).astype(BF16)
    return hi, lo


def _dot_exact_rhs(a, b01):
    hi, lo = _split2(a)
    return (jnp.dot(hi, b01, preferred_element_type=F32)
            + jnp.dot(lo, b01, preferred_element_type=F32))


def _dot_exact_lhs(a01, b):
    hi, lo = _split2(b)
    return (jnp.dot(a01, hi, preferred_element_type=F32)
            + jnp.dot(a01, lo, preferred_element_type=F32))


def _dot3(a, b):
    ah, al = _split2(a)
    bh, bl = _split2(b)
    return (jnp.dot(ah, bh, preferred_element_type=F32)
            + jnp.dot(al, bh, preferred_element_type=F32)
            + jnp.dot(ah, bl, preferred_element_type=F32))


def _sigmoid(x):
    return 1.0 / (1.0 + jnp.exp(-x))


def _silu(x):
    return x * _sigmoid(x)


def _softplus(x):
    return jnp.maximum(x, 0.0) + jnp.log1p(jnp.exp(-jnp.abs(x)))


def _layer_norm(x, g, b):
    mu = jnp.mean(x, axis=-1, keepdims=True)
    xc = x - mu
    var = jnp.mean(xc * xc, axis=-1, keepdims=True)
    return xc * lax.rsqrt(var + LN_EPS) * g + b


def _lower_tri(n, strict=False):
    r = lax.broadcasted_iota(jnp.int32, (n, n), 0)
    c = lax.broadcasted_iota(jnp.int32, (n, n), 1)
    return (r > c) if strict else (r >= c)


def _causal_conv(ext_ref, cur, halo_ref, w, b, first):
    L = cur.shape[0]

    @pl.when(first)
    def _():
        halo_ref[...] = jnp.zeros_like(halo_ref)

    ext_ref[0:HALO, :] = halo_ref[...]
    ext_ref[HALO:HALO + L, :] = cur
    halo_ref[...] = cur[L - HALO:, :]
    acc = b + w[CONV_WIDTH - 1:CONV_WIDTH, :] * cur
    for k in range(CONV_WIDTH - 1):
        off = HALO - (CONV_WIDTH - 1) + k
        acc = acc + w[k:k + 1, :] * ext_ref[off:off + L, :]
    return acc


def _ln_proj_kernel(x_ref, g_ref, b_ref, w_ref, wdt_ref, o_ref, dt_ref, hn_ref):
    j = pl.program_id(1)

    @pl.when(j == 0)
    def _():
        h = _layer_norm(x_ref[...], g_ref[...], b_ref[...])
        hn_ref[...] = h.astype(BF16)
        dt_ref[...] = _dot3(h, wdt_ref[...])

    o_ref[...] = jnp.dot(hn_ref[...], w_ref[...], preferred_element_type=F32)


def _ln_proj(x2, g, b, w_main, w_dt, tm, tn):
    T, D = x2.shape
    N = w_main.shape[1]
    return pl.pallas_call(
        _ln_proj_kernel,
        out_shape=(jax.ShapeDtypeStruct((T, N), F32), jax.ShapeDtypeStruct((T, LANES), F32)),
        grid=(T // tm, N // tn),
        in_specs=[
            pl.BlockSpec((tm, D), lambda i, j: (i, 0)),
            pl.BlockSpec((1, D), lambda i, j: (0, 0)),
            pl.BlockSpec((1, D), lambda i, j: (0, 0)),
            pl.BlockSpec((D, tn), lambda i, j: (0, j)),
            pl.BlockSpec((D, LANES), lambda i, j: (0, 0)),
        ],
        out_specs=(pl.BlockSpec((tm, tn), lambda i, j: (i, j)),
                   pl.BlockSpec((tm, LANES), lambda i, j: (i, 0))),
        scratch_shapes=[pltpu.VMEM((tm, D), BF16)],
        compiler_params=_params(("parallel", "arbitrary")),
        name="ln_proj",
    )(x2, g, b, w_main, w_dt)


def _mlstm_kernel(xm_ref, og_ref, cw_ref, cb_ref, wq_ref, wk_ref, wv_ref, wif_ref, bif_ref,
                  nw_ref, skip_ref, y_ref, ext_ref, halo_ref, ct_ref, n_ref, m_ref):
    L, DM = xm_ref.shape
    H = MLSTM_HEADS
    Dh = DM // H
    first = pl.program_id(1) == 0

    @pl.when(first)
    def _():
        ct_ref[...] = jnp.zeros_like(ct_ref)
        n_ref[...] = jnp.zeros_like(n_ref)
        m_ref[...] = jnp.zeros_like(m_ref)

    xm = xm_ref[...]
    xc = _silu(_causal_conv(ext_ref, xm, halo_ref, cw_ref[...], cb_ref[...], first))
    xcb = xc.astype(BF16)
    xmb = xm.astype(BF16)
    nblk = DM // MXU_DIM
    q = jnp.concatenate([jnp.dot(xcb[:, i * MXU_DIM:(i + 1) * MXU_DIM], wq_ref[i],
                                 preferred_element_type=F32) for i in range(nblk)], axis=1)
    k = jnp.concatenate([jnp.dot(xcb[:, i * MXU_DIM:(i + 1) * MXU_DIM], wk_ref[i],
                                 preferred_element_type=F32) for i in range(nblk)], axis=1)
    v = jnp.concatenate([jnp.dot(xmb[:, i * MXU_DIM:(i + 1) * MXU_DIM], wv_ref[i],
                                 preferred_element_type=F32) for i in range(nblk)], axis=1)
    qb = q.astype(BF16)
    vb = v.astype(BF16)
    gates = (jnp.dot(qb, wif_ref[0], preferred_element_type=F32)
             + jnp.dot(k.astype(BF16), wif_ref[1], preferred_element_type=F32)
             + jnp.dot(vb, wif_ref[2], preferred_element_type=F32)) + bif_ref[...]
    logf = jnp.minimum(gates, 0.0) - jnp.log1p(jnp.exp(-jnp.abs(gates)))
    tri = _lower_tri(L).astype(BF16)
    bcum = _dot_exact_lhs(tri, logf)
    gates_t = gates.T
    bcum_t = bcum.T
    causal = _lower_tri(L)
    ks = k * (1.0 / math.sqrt(Dh))

    outs = []
    for h in range(H):
        sl = slice(h * Dh, (h + 1) * Dh)
        qh = qb[:, sl]
        kh = ks[:, sl]
        vh = vb[:, sl]
        bc = bcum[:, H + h:H + h + 1]
        br = bcum_t[H + h:H + h + 1, :]
        ic = gates[:, h:h + 1]
        ir = gates_t[h:h + 1, :]
        m_prev = m_ref[h:h + 1, 0:1]
        dlog = jnp.where(causal, bc - br + ir, -jnp.inf)
        inter = bc + m_prev
        mt = jnp.maximum(inter, jnp.max(dlog, axis=-1, keepdims=True))
        w_intra = jnp.exp(dlog - mt)
        w_inter = jnp.exp(inter - mt)
        sc = _bdot_nt(qh, kh) * w_intra
        ct = ct_ref[h]
        num = _bdot(sc, vh) + w_inter * _bdot(qh, ct)
        qn = jnp.sum(q[:, sl] * n_ref[h:h + 1, :], axis=-1, keepdims=True)
        den = jnp.sum(sc, axis=-1, keepdims=True) + w_inter * qn
        hb = num / jnp.maximum(jnp.abs(den), jnp.exp(-mt))
        mu = jnp.mean(hb, axis=-1, keepdims=True)
        hc = hb - mu
        var = jnp.mean(hc * hc, axis=-1, keepdims=True)
        outs.append(hc * lax.rsqrt(var + LN_EPS))
        bl = bc[L - 1:L, :]
        wlog = bl - bc + ic
        m_new = jnp.maximum(bl + m_prev, jnp.max(wlog, axis=0, keepdims=True))
        ws = jnp.exp(wlog - m_new)
        decay = jnp.exp(bl + m_prev - m_new)
        kw = kh * ws
        ct_ref[h] = decay * ct + lax.dot_general(
            kw.astype(BF16), vh, (((0,), (0,)), ((), ())), preferred_element_type=F32)
        n_ref[h:h + 1, :] = decay * n_ref[h:h + 1, :] + jnp.sum(kw, axis=0, keepdims=True)
        m_ref[h:h + 1, :] = jnp.broadcast_to(m_new, (1, m_ref.shape[1]))

    hn = jnp.concatenate(outs, axis=1) * nw_ref[...] + skip_ref[...] * xc
    y_ref[...] = (_sigmoid(og_ref[...]) * hn).astype(y_ref.dtype)


def _mlstm(proj, bsz, nc, conv_w, conv_b, wq_bd, wk_bd, wv_bd, wif, bif, norm_w, skip):
    L = CHUNK
    DM = conv_w.shape[1]
    Dh = DM // MLSTM_HEADS
    T = proj.shape[0]
    const2 = lambda b, c: (0, 0)
    const3 = lambda b, c: (0, 0, 0)
    return pl.pallas_call(
        _mlstm_kernel,
        out_shape=jax.ShapeDtypeStruct((T, DM), BF16),
        grid=(bsz, nc),
        in_specs=[
            pl.BlockSpec((L, DM), lambda b, c: (b * nc + c, 0)),
            pl.BlockSpec((L, DM), lambda b, c: (b * nc + c, 1)),
            pl.BlockSpec(conv_w.shape, const2),
            pl.BlockSpec(conv_b.shape, const2),
            pl.BlockSpec(wq_bd.shape, const3),
            pl.BlockSpec(wk_bd.shape, const3),
            pl.BlockSpec(wv_bd.shape, const3),
            pl.BlockSpec(wif.shape, const3),
            pl.BlockSpec(bif.shape, const2),
            pl.BlockSpec(norm_w.shape, const2),
            pl.BlockSpec(skip.shape, const2),
        ],
        out_specs=pl.BlockSpec((L, DM), lambda b, c: (b * nc + c, 0)),
        scratch_shapes=[
            pltpu.VMEM((HALO + L, DM), F32),
            pltpu.VMEM((HALO, DM), F32),
            pltpu.VMEM((MLSTM_HEADS, Dh, Dh), F32),
            pltpu.VMEM((8, Dh), F32),
            pltpu.VMEM((8, LANES), F32),
        ],
        compiler_params=_params(("parallel", "arbitrary")),
        name="mlstm",
    )(proj, proj, conv_w, conv_b, wq_bd, wk_bd, wv_bd, wif, bif, norm_w, skip)


def _ssd_kernel(z_ref, xs_ref, bc_ref, dt_ref, cw_ref, cb_ref, dtb_ref, alog_ref, dskip_ref,
                nw_ref, y_ref, ext_ref, halo_ref, st_ref):
    L, DS = xs_ref.shape
    G, N, P = SSD_GROUPS, SSD_STATE, SSD_HEAD_DIM
    GN = G * N
    DG = DS // G
    first = pl.program_id(1) == 0

    @pl.when(first)
    def _():
        st_ref[...] = jnp.zeros_like(st_ref)

    cur = jnp.concatenate([xs_ref[...], bc_ref[...]], axis=1)
    xbc = _silu(_causal_conv(ext_ref, cur, halo_ref, cw_ref[...], cb_ref[...], first))
    xh = xbc[:, :DS]
    bm = xbc[:, DS:DS + GN]
    cm = xbc[:, DS + GN:]

    dt = _softplus(dt_ref[...] + dtb_ref[...])
    a = -jnp.exp(alog_ref[...])
    adt = dt * a
    tri = _lower_tri(L).astype(BF16)
    a_cs = _dot_exact_lhs(tri, adt)
    a_cs_t = a_cs.T
    ea = jnp.exp(a_cs)
    a_last = a_cs[L - 1:L, :]
    dstate = jnp.exp(a_last - a_cs)
    er = lax.broadcasted_iota(jnp.int32, (LANES, DS), 0)
    ec = lax.broadcasted_iota(jnp.int32, (LANES, DS), 1)
    expand = (ec // P == er).astype(BF16)
    stacked = jnp.concatenate([dt, ea, dstate], axis=0)
    ex = _dot_exact_rhs(stacked, expand)
    dt_x, ea_x, ds_x = ex[:L], ex[L:2 * L], ex[2 * L:]
    xd = xh * dt_x
    xdd = xd * ds_x
    xdb = xd.astype(BF16)
    causal = _lower_tri(L)
    lane = lax.broadcasted_iota(jnp.int32, (L, LANES), 1)
    lo_half = lane < P

    y_parts = []
    heads_per_group = DG // P
    for g in range(G):
        gs = slice(g * N, (g + 1) * N)
        cg = cm[:, gs].astype(BF16)
        bg = bm[:, gs].astype(BF16)
        cbg = _bdot_nt(cg, bg)
        cols = slice(g * DG, (g + 1) * DG)
        st = st_ref[:, cols]
        y_off = jnp.dot(cg, st.astype(BF16), preferred_element_type=F32) * ea_x[:, cols]
        yd = []
        for pr in range(heads_per_group // 2):
            h0 = g * heads_per_group + 2 * pr
            sc = []
            for h in (h0, h0 + 1):
                diff = a_cs[:, h:h + 1] - a_cs_t[h:h + 1, :]
                sc.append((cbg * jnp.exp(jnp.where(causal, diff, -jnp.inf))).astype(BF16))
            lhs = jnp.concatenate(sc, axis=1)
            xp = xdb[:, h0 * P:(h0 + 2) * P]
            zero = jnp.zeros_like(xp)
            rhs = jnp.concatenate([jnp.where(lo_half, xp, zero),
                                   jnp.where(lo_half, zero, xp)], axis=0)
            yd.append(jnp.dot(lhs, rhs, preferred_element_type=F32))
        y_parts.append(jnp.concatenate(yd, axis=1) + y_off)
        st_ref[:, cols] = ea_x[L - 1:L, cols] * st + lax.dot_general(
            bg, xdd[:, cols].astype(BF16), (((0,), (0,)), ((), ())), preferred_element_type=F32)

    y = jnp.concatenate(y_parts, axis=1) + dskip_ref[...] * xh
    y = y * _silu(z_ref[...])
    normed = []
    for g in range(G):
        yg = y[:, g * DG:(g + 1) * DG]
        normed.append(yg * lax.rsqrt(jnp.mean(yg * yg, axis=-1, keepdims=True) + LN_EPS))
    y_ref[...] = (jnp.concatenate(normed, axis=1) * nw_ref[...]).astype(y_ref.dtype)


def _ssd(proj, dt_raw, bsz, nc, d_mlstm, conv_w, conv_b, dt_bias, a_log, dskip_x, norm_w):
    L = CHUNK
    DS = norm_w.shape[1]
    GN2 = 2 * SSD_GROUPS * SSD_STATE
    T = proj.shape[0]
    const2 = lambda b, c: (0, 0)
    z_blk = 2 * d_mlstm // DS
    bc_blk = (2 * d_mlstm + 2 * DS) // GN2
    return pl.pallas_call(
        _ssd_kernel,
        out_shape=jax.ShapeDtypeStruct((T, DS), BF16),
        grid=(bsz, nc),
        in_specs=[
            pl.BlockSpec((L, DS), lambda b, c: (b * nc + c, z_blk)),
            pl.BlockSpec((L, DS), lambda b, c: (b * nc + c, z_blk + 1)),
            pl.BlockSpec((L, GN2), lambda b, c: (b * nc + c, bc_blk)),
            pl.BlockSpec((L, LANES), lambda b, c: (b * nc + c, 0)),
            pl.BlockSpec(conv_w.shape, const2),
            pl.BlockSpec(conv_b.shape, const2),
            pl.BlockSpec(dt_bias.shape, const2),
            pl.BlockSpec(a_log.shape, const2),
            pl.BlockSpec(dskip_x.shape, const2),
            pl.BlockSpec(norm_w.shape, const2),
        ],
        out_specs=pl.BlockSpec((L, DS), lambda b, c: (b * nc + c, 0)),
        scratch_shapes=[
            pltpu.VMEM((HALO + L, DS + GN2), F32),
            pltpu.VMEM((HALO, DS + GN2), F32),
            pltpu.VMEM((SSD_STATE, DS), F32),
        ],
        compiler_params=_params(("parallel", "arbitrary")),
        name="ssd",
    )(proj, proj, proj, dt_raw, conv_w, conv_b, dt_bias, a_log, dskip_x, norm_w)


def _out_ln1_kernel(ym_ref, ys_ref, x_ref, g0_ref, b0_ref, wm_ref, ws_ref, g1_ref, b1_ref,
                    wr_ref, h1_ref, h1b_ref, lg_ref):
    h0 = _layer_norm(x_ref[...], g0_ref[...], b0_ref[...])
    mix = (jnp.dot(ym_ref[...], wm_ref[...], preferred_element_type=F32)
           + jnp.dot(ys_ref[...], ws_ref[...], preferred_element_type=F32))
    h1 = _layer_norm(DEEPNORM_ALPHA * h0 + mix, g1_ref[...], b1_ref[...])
    h1_ref[...] = h1
    h1b_ref[...] = h1.astype(BF16)
    lg_ref[...] = _dot3(h1, wr_ref[...])


def _out_ln1(ym, ys, x2, g0, b0, w_m, w_s, g1, b1, w_r, tm):
    T, D = x2.shape
    DM = ym.shape[1]
    DS = ys.shape[1]
    row = lambda i: (i, 0)
    const = lambda i: (0, 0)
    return pl.pallas_call(
        _out_ln1_kernel,
        out_shape=(jax.ShapeDtypeStruct((T, D), F32), jax.ShapeDtypeStruct((T, D), BF16),
                   jax.ShapeDtypeStruct((T, LANES), F32)),
        grid=(T // tm,),
        in_specs=[
            pl.BlockSpec((tm, DM), row), pl.BlockSpec((tm, DS), row), pl.BlockSpec((tm, D), row),
            pl.BlockSpec((1, D), const), pl.BlockSpec((1, D), const),
            pl.BlockSpec((DM, D), const), pl.BlockSpec((DS, D), const),
            pl.BlockSpec((1, D), const), pl.BlockSpec((1, D), const),
            pl.BlockSpec((D, LANES), const),
        ],
        out_specs=(pl.BlockSpec((tm, D), row), pl.BlockSpec((tm, D), row),
                   pl.BlockSpec((tm, LANES), row)),
        compiler_params=_params(("parallel",)),
        name="out_ln1",
    )(ym, ys, x2, g0, b0, w_m, w_s, g1, b1, w_r)


def _moe_kernel(be_ref, nb_ref, x_ref, wg_ref, wu_ref, wd_ref, rw_ref, o_ref):
    b = pl.program_id(0)
    f = pl.program_id(1)
    nf = pl.num_programs(1)
    live = b < nb_ref[0]

    @pl.when(jnp.logical_not(live))
    def _():
        o_ref[...] = jnp.zeros_like(o_ref)

    @pl.when(live)
    def _():
        xb = x_ref[...]
        a = jnp.dot(xb, wg_ref[0].astype(BF16), preferred_element_type=F32)
        u = jnp.dot(xb, wu_ref[0].astype(BF16), preferred_element_type=F32)
        part = jnp.dot((_silu(a) * u).astype(BF16), wd_ref[0].astype(BF16),
                       preferred_element_type=F32)

        @pl.when(f == 0)
        def _():
            o_ref[...] = part

        @pl.when(jnp.logical_and(f > 0, f < nf - 1))
        def _():
            o_ref[...] += part

        @pl.when(f == nf - 1)
        def _():
            o_ref[...] = (o_ref[...] + part) * rw_ref[...]


def _moe_experts(block_e, n_live, x_sorted, w_gate, w_up, w_down, row_w, nf):
    NR, D = x_sorted.shape
    NB = NR // MOE_BLOCK
    FF = w_gate.shape[2]
    tf = FF // nf

    def live_blk(b, nb):
        return jnp.minimum(b, nb[0] - 1)

    def f_eff(b, f):
        return jnp.where(b % 2 == 0, f, nf - 1 - f)

    return pl.pallas_call(
        _moe_kernel,
        out_shape=jax.ShapeDtypeStruct((NR, D), F32),
        grid_spec=pltpu.PrefetchScalarGridSpec(
            num_scalar_prefetch=2,
            grid=(NB, nf),
            in_specs=[
                pl.BlockSpec((MOE_BLOCK, D), lambda b, f, be, nb: (live_blk(b, nb), 0)),
                pl.BlockSpec((1, D, tf), lambda b, f, be, nb: (be[live_blk(b, nb)], 0, f_eff(b, f))),
                pl.BlockSpec((1, D, tf), lambda b, f, be, nb: (be[live_blk(b, nb)], 0, f_eff(b, f))),
                pl.BlockSpec((1, tf, D), lambda b, f, be, nb: (be[live_blk(b, nb)], f_eff(b, f), 0)),
                pl.BlockSpec((MOE_BLOCK, 1), lambda b, f, be, nb: (b, 0)),
            ],
            out_specs=pl.BlockSpec((MOE_BLOCK, D), lambda b, f, be, nb: (b, 0)),
        ),
        compiler_params=_params(("arbitrary", "arbitrary")),
        name="moe_experts",
    )(block_e, n_live, x_sorted, w_gate, w_up, w_down, row_w)


def _combine_ln2_kernel(h_ref, m0_ref, m1_ref, g_ref, b_ref, o_ref):
    y = DEEPNORM_ALPHA * h_ref[...] + (m0_ref[...] + m1_ref[...])
    o_ref[...] = _layer_norm(y, g_ref[...], b_ref[...])


def _combine_ln2(h1, m0, m1, g, b, tm):
    T, D = h1.shape
    row = lambda i: (i, 0)
    const = lambda i: (0, 0)
    return pl.pallas_call(
        _combine_ln2_kernel,
        out_shape=jax.ShapeDtypeStruct((T, D), F32),
        grid=(T // tm,),
        in_specs=[pl.BlockSpec((tm, D), row), pl.BlockSpec((tm, D), row), pl.BlockSpec((tm, D), row),
                  pl.BlockSpec((1, D), const), pl.BlockSpec((1, D), const)],
        out_specs=pl.BlockSpec((tm, D), row),
        compiler_params=_params(("parallel",)),
        name="combine_ln2",
    )(h1, m0, m1, g, b)


def _block_diag_tiles(w):
    nb = w.shape[0]
    per = MXU_DIM // QKV_BLOCK
    wt = w.reshape(nb // per, per, QKV_BLOCK, QKV_BLOCK)
    eye = jnp.eye(per, dtype=w.dtype)
    t = jnp.einsum('inde,nm->indme', wt, eye)
    return t.reshape(nb // per, MXU_DIM, MXU_DIM).astype(BF16)


def _pad_lanes(a, width=LANES):
    return jnp.pad(a, [(0, 0)] * (a.ndim - 1) + [(0, width - a.shape[-1])])


def _route(logits, b_rc, b_rf):
    T = logits.shape[0]
    pc = jax.nn.softmax(logits[:, :MOE_GROUPS] + b_rc, axis=-1)
    gidx = jnp.argmax(pc, axis=-1)
    pg = jnp.max(pc, axis=-1)
    lf = (logits[:, MOE_GROUPS:MOE_GROUPS + MOE_GROUPS * EXPERTS_PER_GROUP] + b_rf)
    lf = lf.reshape(T, MOE_GROUPS, EXPERTS_PER_GROUP)
    sel = jnp.take_along_axis(lf, gidx[:, None, None], axis=1)[:, 0]
    pf = jax.nn.softmax(sel, axis=-1)
    v2, i2 = lax.top_k(pf, TOP_K)
    wts = v2 / v2.sum(-1, keepdims=True) * pg[:, None]
    eid = gidx[:, None].astype(jnp.int32) * EXPERTS_PER_GROUP + i2.astype(jnp.int32)
    return eid, wts


def _dispatch(eid, wts, n_experts):
    T = eid.shape[0]
    TK = T * TOP_K
    NB = TK // MOE_BLOCK + n_experts + 1
    NR = NB * MOE_BLOCK
    e_flat = eid.reshape(TK)
    w_flat = wts.reshape(TK)
    onehot = (e_flat[:, None] == jnp.arange(n_experts, dtype=jnp.int32)[None, :]).astype(jnp.int32)
    csum = jnp.cumsum(onehot, axis=0)
    counts = csum[-1]
    rank = jnp.sum(csum * onehot, axis=1) - 1
    padded = (counts + MOE_BLOCK - 1) // MOE_BLOCK * MOE_BLOCK
    pad_end = jnp.cumsum(padded)
    pad_start = pad_end - padded
    dest = pad_start[e_flat] + rank
    tok_flat = jnp.repeat(jnp.arange(T, dtype=jnp.int32), TOP_K)
    row_tok = jnp.full((NR,), T, jnp.int32).at[dest].set(tok_flat)
    row_w = jnp.zeros((NR,), F32).at[dest].set(w_flat)
    block_start = jnp.arange(NB, dtype=jnp.int32) * MOE_BLOCK
    block_e = jnp.minimum(jnp.searchsorted(pad_end, block_start, side='right'),
                          n_experts - 1).astype(jnp.int32)
    n_live = (pad_end[-1] // MOE_BLOCK).astype(jnp.int32).reshape(1)
    return dest.reshape(T, TOP_K), row_tok, row_w, block_e, n_live


def kernel(x, ln_in_g, ln_in_b, w_in, conv_m_w, conv_m_b, w_q, w_k, w_v, w_if, b_if, mlstm_norm_w, mlstm_skip, conv_s_w, conv_s_b, dt_bias, a_log, d_skip, ssd_norm_w, w_out, ln1_g, ln1_b, w_router_coarse, b_router_coarse, w_router_fine, b_router_fine, w_gate_e, w_up_e, w_down_e, ln2_g, ln2_b):
    bsz, seq, D = x.shape
    T = bsz * seq
    nc = seq // CHUNK
    DM = conv_m_w.shape[-1]
    DS = ssd_norm_w.shape[-1]
    n_heads_s = DS // SSD_HEAD_DIM
    n_experts = w_gate_e.shape[1]
    assert w_in.shape[0] == DEPTH
    l = 0
    x2 = x.reshape(T, D)
    row = lambda a: a.reshape(1, -1)

    n_main = w_in.shape[-1] - n_heads_s
    w_main = w_in[l, :, :n_main].astype(BF16)
    w_dt = _pad_lanes(w_in[l, :, n_main:])
    proj, dt_raw = _ln_proj(x2, row(ln_in_g), row(ln_in_b), w_main, w_dt, tm=1024, tn=1024)

    wif = _pad_lanes(w_if[l]).astype(BF16).reshape(3, DM, LANES)
    y_m = _mlstm(proj, bsz, nc, conv_m_w[l], row(conv_m_b[l]),
                 _block_diag_tiles(w_q[l]), _block_diag_tiles(w_k[l]), _block_diag_tiles(w_v[l]),
                 wif, _pad_lanes(row(b_if[l])), row(mlstm_norm_w[l]), row(mlstm_skip[l]))

    y_s = _ssd(proj, dt_raw, bsz, nc, DM, conv_s_w[l], row(conv_s_b[l]),
               _pad_lanes(row(dt_bias[l])), _pad_lanes(row(a_log[l])),
               row(jnp.repeat(d_skip[l], SSD_HEAD_DIM)), row(ssd_norm_w[l]))

    w_o = w_out[l].astype(BF16)
    w_r = _pad_lanes(jnp.concatenate([w_router_coarse[l], w_router_fine[l]], axis=1))
    h1, h1b, logits = _out_ln1(y_m, y_s, x2, row(ln_in_g), row(ln_in_b), w_o[:DM], w_o[DM:],
                               row(ln1_g[l]), row(ln1_b[l]), w_r, tm=256)

    eid, wts = _route(logits, b_router_coarse[l], b_router_fine[l])
    dest, row_tok, row_w, block_e, n_live = _dispatch(eid, wts, n_experts)
    x_sorted = jnp.take(h1b, row_tok, axis=0, mode='fill', fill_value=0)
    yb = _moe_experts(block_e, n_live, x_sorted, w_gate_e[l], w_up_e[l], w_down_e[l],
                      row_w.reshape(-1, 1), nf=2)

    m0 = jnp.take(yb, dest[:, 0], axis=0)
    m1 = jnp.take(yb, dest[:, 1], axis=0)
    out = _combine_ln2(h1, m0, m1, row(ln2_g[l]), row(ln2_b[l]), tm=512)
    return out.reshape(bsz, seq, D)
```

```python
import functools
import math

import jax
import jax.numpy as jnp
from jax import lax
from jax.experimental import pallas as pl
from jax.experimental.pallas import tpu as pltpu

F32 = jnp.float32
BF16 = jnp.bfloat16

LANES = 128
MXU_DIM = 256
VMEM_LIMIT = 56 * 1024 * 1024

MLSTM_HEADS = 4
QKV_BLOCK = 4
CHUNK = 128
SSD_HEAD_DIM = 64
SSD_GROUPS = 4
SSD_STATE = 128
CONV_WIDTH = 4
MOE_GROUPS = 8
EXPERTS_PER_GROUP = 8
TOP_K = 2
MOE_BLOCK = 128
MOE_ITEM_BLOCKS = 4
MOE_FF_TILE = 256
DEPTH = 1
DEEPNORM_ALPHA = (2 * DEPTH) ** 0.25
LN_EPS = 1e-5
HALO = 8


def _params(semantics):
    return pltpu.CompilerParams(dimension_semantics=semantics, vmem_limit_bytes=VMEM_LIMIT)


def _bdot(a, b):
    return jnp.dot(a.astype(BF16), b.astype(BF16), preferred_element_type=F32)


def _bdot_nt(a, b):
    return lax.dot_general(a.astype(BF16), b.astype(BF16), (((1,), (1,)), ((), ())),
                           preferred_element_type=F32)


def _split2(a):
    hi = a.astype(BF16)
    lo = (a - hi.astype(F32)).astype(BF16)
    return hi, lo


def _dot_exact_rhs(a, b01):
    hi, lo = _split2(a)
    return (jnp.dot(hi, b01, preferred_element_type=F32)
            + jnp.dot(lo, b01, preferred_element_type=F32))


def _dot_exact_lhs(a01, b):
    hi, lo = _split2(b)
    return (jnp.dot(a01, hi, preferred_element_type=F32)
            + jnp.dot(a01, lo, preferred_element_type=F32))


def _dot3(a, b):
    ah, al = _split2(a)
    bh, bl = _split2(b)
    return (jnp.dot(ah, bh, preferred_element_type=F32)
            + jnp.dot(al, bh, preferred_element_type=F32)
            + jnp.dot(ah, bl, preferred_element_type=F32))


def _sigmoid(x):
    return 1.0 / (1.0 + jnp.exp(-x))


def _silu(x):
    return x * _sigmoid(x)


def _softplus(x):
    return jnp.maximum(x, 0.0) + jnp.log1p(jnp.exp(-jnp.abs(x)))


def _layer_norm(x, g, b):
    mu = jnp.mean(x, axis=-1, keepdims=True)
    xc = x - mu
    var = jnp.mean(xc * xc, axis=-1, keepdims=True)
    return xc * lax.rsqrt(var + LN_EPS) * g + b


def _lower_tri(n, strict=False):
    r = lax.broadcasted_iota(jnp.int32, (n, n), 0)
    c = lax.broadcasted_iota(jnp.int32, (n, n), 1)
    return (r > c) if strict else (r >= c)


def _causal_conv(ext_ref, cur, halo_ref, w, b, first):
    L = cur.shape[0]

    @pl.when(first)
    def _():
        halo_ref[...] = jnp.zeros_like(halo_ref)

    ext_ref[0:HALO, :] = halo_ref[...]
    ext_ref[HALO:HALO + L, :] = cur
    halo_ref[...] = cur[L - HALO:, :]
    acc = b + w[CONV_WIDTH - 1:CONV_WIDTH, :] * cur
    for k in range(CONV_WIDTH - 1):
        off = HALO - (CONV_WIDTH - 1) + k
        acc = acc + w[k:k + 1, :] * ext_ref[off:off + L, :]
    return acc


def _ln_proj_kernel(x_ref, g_ref, b_ref, w_ref, wdt_ref, o_ref, dt_ref, hn_ref):
    j = pl.program_id(1)

    @pl.when(j == 0)
    def _():
        h = _layer_norm(x_ref[...], g_ref[...], b_ref[...])
        hn_ref[...] = h.astype(BF16)
        dt_ref[...] = _dot3(h, wdt_ref[...])

    o_ref[...] = jnp.dot(hn_ref[...], w_ref[...], preferred_element_type=F32)


def _ln_proj(x2, g, b, w_main, w_dt, tm, tn):
    T, D = x2.shape
    N = w_main.shape[1]
    return pl.pallas_call(
        _ln_proj_kernel,
        out_shape=(jax.ShapeDtypeStruct((T, N), F32), jax.ShapeDtypeStruct((T, LANES), F32)),
        grid=(T // tm, N // tn),
        in_specs=[
            pl.BlockSpec((tm, D), lambda i, j: (i, 0)),
            pl.BlockSpec((1, D), lambda i, j: (0, 0)),
            pl.BlockSpec((1, D), lambda i, j: (0, 0)),
            pl.BlockSpec((D, tn), lambda i, j: (0, j)),
            pl.BlockSpec((D, LANES), lambda i, j: (0, 0)),
        ],
        out_specs=(pl.BlockSpec((tm, tn), lambda i, j: (i, j)),
                   pl.BlockSpec((tm, LANES), lambda i, j: (i, 0))),
        scratch_shapes=[pltpu.VMEM((tm, D), BF16)],
        compiler_params=_params(("parallel", "arbitrary")),
        name="ln_proj",
    )(x2, g, b, w_main, w_dt)


def _mlstm_kernel(xm_ref, og_ref, cw_ref, cb_ref, wq_ref, wk_ref, wv_ref, wif_ref, bif_ref,
                  nw_ref, skip_ref, y_ref, ext_ref, halo_ref, ct_ref, n_ref, m_ref):
    L, DM = xm_ref.shape
    H = MLSTM_HEADS
    Dh = DM // H
    first = pl.program_id(1) == 0

    @pl.when(first)
    def _():
        ct_ref[...] = jnp.zeros_like(ct_ref)
        n_ref[...] = jnp.zeros_like(n_ref)
        m_ref[...] = jnp.zeros_like(m_ref)

    xm = xm_ref[...]
    xc = _silu(_causal_conv(ext_ref, xm, halo_ref, cw_ref[...], cb_ref[...], first))
    xcb = xc.astype(BF16)
    xmb = xm.astype(BF16)
    nblk = DM // MXU_DIM
    q = jnp.concatenate([jnp.dot(xcb[:, i * MXU_DIM:(i + 1) * MXU_DIM], wq_ref[i],
                                 preferred_element_type=F32) for i in range(nblk)], axis=1)
    k = jnp.concatenate([jnp.dot(xcb[:, i * MXU_DIM:(i + 1) * MXU_DIM], wk_ref[i],
                                 preferred_element_type=F32) for i in range(nblk)], axis=1)
    v = jnp.concatenate([jnp.dot(xmb[:, i * MXU_DIM:(i + 1) * MXU_DIM], wv_ref[i],
                                 preferred_element_type=F32) for i in range(nblk)], axis=1)
    qb = q.astype(BF16)
    vb = v.astype(BF16)
    gates = (jnp.dot(qb, wif_ref[0], preferred_element_type=F32)
             + jnp.dot(k.astype(BF16), wif_ref[1], preferred_element_type=F32)
             + jnp.dot(vb, wif_ref[2], preferred_element_type=F32)) + bif_ref[...]
    logf = jnp.minimum(gates, 0.0) - jnp.log1p(jnp.exp(-jnp.abs(gates)))
    tri = _lower_tri(L).astype(BF16)
    bcum = _dot_exact_lhs(tri, logf)
    gates_t = gates.T
    bcum_t = bcum.T
    causal = _lower_tri(L)
    ks = k * (1.0 / math.sqrt(Dh))

    outs = []
    for h in range(H):
        sl = slice(h * Dh, (h + 1) * Dh)
        qh = qb[:, sl]
        kh = ks[:, sl]
        vh = vb[:, sl]
        bc = bcum[:, H + h:H + h + 1]
        br = bcum_t[H + h:H + h + 1, :]
        ic = gates[:, h:h + 1]
        ir = gates_t[h:h + 1, :]
        m_prev = m_ref[h:h + 1, 0:1]
        dlog = jnp.where(causal, bc - br + ir, -jnp.inf)
        inter = bc + m_prev
        mt = jnp.maximum(inter, jnp.max(dlog, axis=-1, keepdims=True))
        w_intra = jnp.exp(dlog - mt)
        w_inter = jnp.exp(inter - mt)
        sc = _bdot_nt(qh, kh) * w_intra
        ct = ct_ref[h]
        num = _bdot(sc, vh) + w_inter * _bdot(qh, ct)
        qn = jnp.sum(q[:, sl] * n_ref[h:h + 1, :], axis=-1, keepdims=True)
        den = jnp.sum(sc, axis=-1, keepdims=True) + w_inter * qn
        hb = num / jnp.maximum(jnp.abs(den), jnp.exp(-mt))
        mu = jnp.mean(hb, axis=-1, keepdims=True)
        hc = hb - mu
        var = jnp.mean(hc * hc, axis=-1, keepdims=True)
        outs.append(hc * lax.rsqrt(var + LN_EPS))
        bl = bc[L - 1:L, :]
        wlog = bl - bc + ic
        m_new = jnp.maximum(bl + m_prev, jnp.max(wlog, axis=0, keepdims=True))
        ws = jnp.exp(wlog - m_new)
        decay = jnp.exp(bl + m_prev - m_new)
        kw = kh * ws
        ct_ref[h] = decay * ct + lax.dot_general(
            kw.astype(BF16), vh, (((0,), (0,)), ((), ())), preferred_element_type=F32)
        n_ref[h:h + 1, :] = decay * n_ref[h:h + 1, :] + jnp.sum(kw, axis=0, keepdims=True)
        m_ref[h:h + 1, :] = jnp.broadcast_to(m_new, (1, m_ref.shape[1]))

    hn = jnp.concatenate(outs, axis=1) * nw_ref[...] + skip_ref[...] * xc
    y_ref[...] = (_sigmoid(og_ref[...]) * hn).astype(y_ref.dtype)


def _mlstm(proj, bsz, nc, conv_w, conv_b, wq_bd, wk_bd, wv_bd, wif, bif, norm_w, skip):
    L = CHUNK
    DM = conv_w.shape[1]
    Dh = DM // MLSTM_HEADS
    T = proj.shape[0]
    const2 = lambda b, c: (0, 0)
    const3 = lambda b, c: (0, 0, 0)
    return pl.pallas_call(
        _mlstm_kernel,
        out_shape=jax.ShapeDtypeStruct((T, DM), BF16),
        grid=(bsz, nc),
        in_specs=[
            pl.BlockSpec((L, DM), lambda b, c: (b * nc + c, 0)),
            pl.BlockSpec((L, DM), lambda b, c: (b * nc + c, 1)),
            pl.BlockSpec(conv_w.shape, const2),
            pl.BlockSpec(conv_b.shape, const2),
            pl.BlockSpec(wq_bd.shape, const3),
            pl.BlockSpec(wk_bd.shape, const3),
            pl.BlockSpec(wv_bd.shape, const3),
            pl.BlockSpec(wif.shape, const3),
            pl.BlockSpec(bif.shape, const2),
            pl.BlockSpec(norm_w.shape, const2),
            pl.BlockSpec(skip.shape, const2),
        ],
        out_specs=pl.BlockSpec((L, DM), lambda b, c: (b * nc + c, 0)),
        scratch_shapes=[
            pltpu.VMEM((HALO + L, DM), F32),
            pltpu.VMEM((HALO, DM), F32),
            pltpu.VMEM((MLSTM_HEADS, Dh, Dh), F32),
            pltpu.VMEM((8, Dh), F32),
            pltpu.VMEM((8, LANES), F32),
        ],
        compiler_params=_params(("parallel", "arbitrary")),
        name="mlstm",
    )(proj, proj, conv_w, conv_b, wq_bd, wk_bd, wv_bd, wif, bif, norm_w, skip)


def _ssd_kernel(z_ref, xs_ref, bc_ref, dt_ref, cw_ref, cb_ref, dtb_ref, alog_ref, dskip_ref,
                nw_ref, y_ref, ext_ref, halo_ref, st_ref):
    L, DS = xs_ref.shape
    G, N, P = SSD_GROUPS, SSD_STATE, SSD_HEAD_DIM
    GN = G * N
    DG = DS // G
    first = pl.program_id(1) == 0

    @pl.when(first)
    def _():
        st_ref[...] = jnp.zeros_like(st_ref)

    cur = jnp.concatenate([xs_ref[...], bc_ref[...]], axis=1)
    xbc = _silu(_causal_conv(ext_ref, cur, halo_ref, cw_ref[...], cb_ref[...], first))
    xh = xbc[:, :DS]
    bm = xbc[:, DS:DS + GN]
    cm = xbc[:, DS + GN:]

    dt = _softplus(dt_ref[...] + dtb_ref[...])
    a = -jnp.exp(alog_ref[...])
    adt = dt * a
    tri = _lower_tri(L).astype(BF16)
    a_cs = _dot_exact_lhs(tri, adt)
    a_cs_t = a_cs.T
    ea = jnp.exp(a_cs)
    a_last = a_cs[L - 1:L, :]
    dstate = jnp.exp(a_last - a_cs)
    er = lax.broadcasted_iota(jnp.int32, (LANES, DS), 0)
    ec = lax.broadcasted_iota(jnp.int32, (LANES, DS), 1)
    expand = (ec // P == er).astype(BF16)
    stacked = jnp.concatenate([dt, ea, dstate], axis=0)
    ex = _dot_exact_rhs(stacked, expand)
    dt_x, ea_x, ds_x = ex[:L], ex[L:2 * L], ex[2 * L:]
    xd = xh * dt_x
    xdd = xd * ds_x
    xdb = xd.astype(BF16)
    causal = _lower_tri(L)
    lane = lax.broadcasted_iota(jnp.int32, (L, LANES), 1)
    lo_half = lane < P

    y_parts = []
    heads_per_group = DG // P
    for g in range(G):
        gs = slice(g * N, (g + 1) * N)
        cg = cm[:, gs].astype(BF16)
        bg = bm[:, gs].astype(BF16)
        cbg = _bdot_nt(cg, bg)
        cols = slice(g * DG, (g + 1) * DG)
        st = st_ref[:, cols]
        y_off = jnp.dot(cg, st.astype(BF16), preferred_element_type=F32) * ea_x[:, cols]
        yd = []
        for pr in range(heads_per_group // 2):
            h0 = g * heads_per_group + 2 * pr
            sc = []
            for h in (h0, h0 + 1):
                diff = a_cs[:, h:h + 1] - a_cs_t[h:h + 1, :]
                sc.append((cbg * jnp.exp(jnp.where(causal, diff, -jnp.inf))).astype(BF16))
            lhs = jnp.concatenate(sc, axis=1)
            xp = xdb[:, h0 * P:(h0 + 2) * P]
            zero = jnp.zeros_like(xp)
            rhs = jnp.concatenate([jnp.where(lo_half, xp, zero),
                                   jnp.where(lo_half, zero, xp)], axis=0)
            yd.append(jnp.dot(lhs, rhs, preferred_element_type=F32))
        y_parts.append(jnp.concatenate(yd, axis=1) + y_off)
        st_ref[:, cols] = ea_x[L - 1:L, cols] * st + lax.dot_general(
            bg, xdd[:, cols].astype(BF16), (((0,), (0,)), ((), ())), preferred_element_type=F32)

    y = jnp.concatenate(y_parts, axis=1) + dskip_ref[...] * xh
    y = y * _silu(z_ref[...])
    normed = []
    for g in range(G):
        yg = y[:, g * DG:(g + 1) * DG]
        normed.append(yg * lax.rsqrt(jnp.mean(yg * yg, axis=-1, keepdims=True) + LN_EPS))
    y_ref[...] = (jnp.concatenate(normed, axis=1) * nw_ref[...]).astype(y_ref.dtype)


def _ssd(proj, dt_raw, bsz, nc, d_mlstm, conv_w, conv_b, dt_bias, a_log, dskip_x, norm_w):
    L = CHUNK
    DS = norm_w.shape[1]
    GN2 = 2 * SSD_GROUPS * SSD_STATE
    T = proj.shape[0]
    const2 = lambda b, c: (0, 0)
    z_blk = 2 * d_mlstm // DS
    bc_blk = (2 * d_mlstm + 2 * DS) // GN2
    return pl.pallas_call(
        _ssd_kernel,
        out_shape=jax.ShapeDtypeStruct((T, DS), BF16),
        grid=(bsz, nc),
        in_specs=[
            pl.BlockSpec((L, DS), lambda b, c: (b * nc + c, z_blk)),
            pl.BlockSpec((L, DS), lambda b, c: (b * nc + c, z_blk + 1)),
            pl.BlockSpec((L, GN2), lambda b, c: (b * nc + c, bc_blk)),
            pl.BlockSpec((L, LANES), lambda b, c: (b * nc + c, 0)),
            pl.BlockSpec(conv_w.shape, const2),
            pl.BlockSpec(conv_b.shape, const2),
            pl.BlockSpec(dt_bias.shape, const2),
            pl.BlockSpec(a_log.shape, const2),
            pl.BlockSpec(dskip_x.shape, const2),
            pl.BlockSpec(norm_w.shape, const2),
        ],
        out_specs=pl.BlockSpec((L, DS), lambda b, c: (b * nc + c, 0)),
        scratch_shapes=[
            pltpu.VMEM((HALO + L, DS + GN2), F32),
            pltpu.VMEM((HALO, DS + GN2), F32),
            pltpu.VMEM((SSD_STATE, DS), F32),
        ],
        compiler_params=_params(("parallel", "arbitrary")),
        name="ssd",
    )(proj, proj, proj, dt_raw, conv_w, conv_b, dt_bias, a_log, dskip_x, norm_w)


def _out_ln1_kernel(ym_ref, ys_ref, x_ref, g0_ref, b0_ref, wm_ref, ws_ref, g1_ref, b1_ref,
                    wr_ref, h1_ref, lg_ref):
    h0 = _layer_norm(x_ref[...], g0_ref[...], b0_ref[...])
    mix = (jnp.dot(ym_ref[...], wm_ref[...], preferred_element_type=F32)
           + jnp.dot(ys_ref[...], ws_ref[...], preferred_element_type=F32))
    h1 = _layer_norm(DEEPNORM_ALPHA * h0 + mix, g1_ref[...], b1_ref[...])
    h1_ref[...] = h1
    lg_ref[...] = _dot3(h1, wr_ref[...])


def _out_ln1(ym, ys, x2, g0, b0, w_m, w_s, g1, b1, w_r, tm):
    T, D = x2.shape
    DM = ym.shape[1]
    DS = ys.shape[1]
    row = lambda i: (i, 0)
    const = lambda i: (0, 0)
    return pl.pallas_call(
        _out_ln1_kernel,
        out_shape=(jax.ShapeDtypeStruct((T, D), F32), jax.ShapeDtypeStruct((T, LANES), F32)),
        grid=(T // tm,),
        in_specs=[
            pl.BlockSpec((tm, DM), row), pl.BlockSpec((tm, DS), row), pl.BlockSpec((tm, D), row),
            pl.BlockSpec((1, D), const), pl.BlockSpec((1, D), const),
            pl.BlockSpec((DM, D), const), pl.BlockSpec((DS, D), const),
            pl.BlockSpec((1, D), const), pl.BlockSpec((1, D), const),
            pl.BlockSpec((D, LANES), const),
        ],
        out_specs=(pl.BlockSpec((tm, D), row), pl.BlockSpec((tm, LANES), row)),
        compiler_params=_params(("parallel",)),
        name="out_ln1",
    )(ym, ys, x2, g0, b0, w_m, w_s, g1, b1, w_r)


def _moe_kernel(ie_ref, ir_ref, inb_ref, nit_ref, x_hbm, wg_ref, wu_ref, wd_ref, o_hbm,
                xbuf, acc, wgb, wub, wdb, xsem, osem):
    i = pl.program_id(0)
    f = pl.program_id(1)
    nf = pl.num_programs(1)
    n_items = nit_ref[0]
    live = i < n_items
    slot = i % 2

    def x_copy(item, s, j):
        rows = pl.ds(pl.multiple_of(ir_ref[item] + j * MOE_BLOCK, MOE_BLOCK), MOE_BLOCK)
        return pltpu.make_async_copy(x_hbm.at[rows], xbuf.at[s, pl.ds(j * MOE_BLOCK, MOE_BLOCK)],
                                     xsem.at[s])

    def o_copy(item, s, j):
        rows = pl.ds(pl.multiple_of(ir_ref[item] + j * MOE_BLOCK, MOE_BLOCK), MOE_BLOCK)
        return pltpu.make_async_copy(acc.at[s, pl.ds(j * MOE_BLOCK, MOE_BLOCK)], o_hbm.at[rows],
                                     osem.at[s])

    def for_blocks(item, fn):
        n = inb_ref[item]
        for j in range(MOE_ITEM_BLOCKS):
            pl.when(j < n)(functools.partial(fn, j))

    @pl.when(jnp.logical_and(live, f == 0))
    def _():
        @pl.when(i == 0)
        def _():
            for_blocks(i, lambda j: x_copy(i, slot, j).start())

        @pl.when(i + 1 < n_items)
        def _():
            for_blocks(i + 1, lambda j: x_copy(i + 1, 1 - slot, j).start())

        for_blocks(i, lambda j: x_copy(i, slot, j).wait())

        @pl.when(i >= 2)
        def _():
            for_blocks(i - 2, lambda j: o_copy(i - 2, slot, j).wait())

    @pl.when(live)
    def _():
        wgb[...] = wg_ref[0].astype(BF16)
        wub[...] = wu_ref[0].astype(BF16)
        wdb[...] = wd_ref[0].astype(BF16)

        def block(j):
            rows = pl.ds(j * MOE_BLOCK, MOE_BLOCK)
            xb = xbuf[slot, rows, :].astype(BF16)
            a = jnp.dot(xb, wgb[...], preferred_element_type=F32)
            u = jnp.dot(xb, wub[...], preferred_element_type=F32)
            part = jnp.dot((_silu(a) * u).astype(BF16), wdb[...], preferred_element_type=F32)

            @pl.when(f == 0)
            def _():
                acc[slot, rows, :] = part

            @pl.when(f > 0)
            def _():
                acc[slot, rows, :] += part

        for_blocks(i, block)

    @pl.when(jnp.logical_and(live, f == nf - 1))
    def _():
        for_blocks(i, lambda j: o_copy(i, slot, j).start())

        @pl.when(i == n_items - 1)
        def _():
            for_blocks(i, lambda j: o_copy(i, slot, j).wait())

            @pl.when(i >= 1)
            def _():
                for_blocks(i - 1, lambda j: o_copy(i - 1, 1 - slot, j).wait())

            zrows = pl.ds(0, MOE_BLOCK)
            xbuf[slot, zrows, :] = jnp.zeros((MOE_BLOCK, xbuf.shape[2]), xbuf.dtype)

            def z_copy(b):
                rows = pl.ds(pl.multiple_of(b * MOE_BLOCK, MOE_BLOCK), MOE_BLOCK)
                return pltpu.make_async_copy(xbuf.at[slot, zrows], o_hbm.at[rows], osem.at[slot])

            first, last = nit_ref[1], o_hbm.shape[0] // MOE_BLOCK
            lax.fori_loop(first, last, lambda b, c: (z_copy(b).start(), c)[1], 0)
            lax.fori_loop(first, last, lambda b, c: (z_copy(b).wait(), c)[1], 0)


def _moe_experts(item_e, item_row0, item_nblk, n_items, x_sorted, w_gate, w_up, w_down):
    NR, D = x_sorted.shape
    NI = item_e.shape[0]
    FF = w_gate.shape[2]
    tf = MOE_FF_TILE
    nf = FF // tf
    cap = MOE_ITEM_BLOCKS * MOE_BLOCK

    def expert(i, ie, nit):
        return ie[jnp.minimum(i, nit[0] - 1)]

    def ff(i, f, nit):
        return jnp.where(i < nit[0], f, nf - 1)

    return pl.pallas_call(
        _moe_kernel,
        out_shape=jax.ShapeDtypeStruct((NR, D), F32),
        grid_spec=pltpu.PrefetchScalarGridSpec(
            num_scalar_prefetch=4,
            grid=(NI, nf),
            in_specs=[
                pl.BlockSpec(memory_space=pl.ANY),
                pl.BlockSpec((1, D, tf),
                             lambda i, f, ie, ir, inb, nit: (expert(i, ie, nit), 0, ff(i, f, nit))),
                pl.BlockSpec((1, D, tf),
                             lambda i, f, ie, ir, inb, nit: (expert(i, ie, nit), 0, ff(i, f, nit))),
                pl.BlockSpec((1, tf, D),
                             lambda i, f, ie, ir, inb, nit: (expert(i, ie, nit), ff(i, f, nit), 0)),
            ],
            out_specs=pl.BlockSpec(memory_space=pl.ANY),
            scratch_shapes=[
                pltpu.VMEM((2, cap, D), F32),
                pltpu.VMEM((2, cap, D), F32),
                pltpu.VMEM((D, tf), BF16),
                pltpu.VMEM((D, tf), BF16),
                pltpu.VMEM((tf, D), BF16),
                pltpu.SemaphoreType.DMA((2,)),
                pltpu.SemaphoreType.DMA((2,)),
            ],
        ),
        compiler_params=_params(("arbitrary", "arbitrary")),
        name="moe_experts",
    )(item_e, item_row0, item_nblk, n_items, x_sorted, w_gate, w_up, w_down)


def _combine_ln2_kernel(h_ref, m0_ref, m1_ref, w_ref, g_ref, b_ref, o_ref):
    w = w_ref[...]
    moe = m0_ref[...] * w[:, 0:1] + m1_ref[...] * w[:, 1:2]
    o_ref[...] = _layer_norm(DEEPNORM_ALPHA * h_ref[...] + moe, g_ref[...], b_ref[...])


def _combine_ln2(h1, m0, m1, wts, g, b, tm):
    T, D = h1.shape
    row = lambda i: (i, 0)
    const = lambda i: (0, 0)
    return pl.pallas_call(
        _combine_ln2_kernel,
        out_shape=jax.ShapeDtypeStruct((T, D), F32),
        grid=(T // tm,),
        in_specs=[pl.BlockSpec((tm, D), row), pl.BlockSpec((tm, D), row), pl.BlockSpec((tm, D), row),
                  pl.BlockSpec((tm, TOP_K), row),
                  pl.BlockSpec((1, D), const), pl.BlockSpec((1, D), const)],
        out_specs=pl.BlockSpec((tm, D), row),
        compiler_params=_params(("parallel",)),
        name="combine_ln2",
    )(h1, m0, m1, wts, g, b)


def _block_diag_tiles(w):
    nb = w.shape[0]
    per = MXU_DIM // QKV_BLOCK
    wt = w.reshape(nb // per, per, QKV_BLOCK, QKV_BLOCK)
    eye = jnp.eye(per, dtype=w.dtype)
    t = jnp.einsum('inde,nm->indme', wt, eye)
    return t.reshape(nb // per, MXU_DIM, MXU_DIM).astype(BF16)


def _pad_lanes(a, width=LANES):
    return jnp.pad(a, [(0, 0)] * (a.ndim - 1) + [(0, width - a.shape[-1])])


def _route(logits, b_rc, b_rf):
    T = logits.shape[0]
    pc = jax.nn.softmax(logits[:, :MOE_GROUPS] + b_rc, axis=-1)
    gidx = jnp.argmax(pc, axis=-1)
    pg = jnp.max(pc, axis=-1)
    lf = (logits[:, MOE_GROUPS:MOE_GROUPS + MOE_GROUPS * EXPERTS_PER_GROUP] + b_rf)
    lf = lf.reshape(T, MOE_GROUPS, EXPERTS_PER_GROUP)
    sel = jnp.take_along_axis(lf, gidx[:, None, None], axis=1)[:, 0]
    pf = jax.nn.softmax(sel, axis=-1)
    v2, i2 = lax.top_k(pf, TOP_K)
    wts = v2 / v2.sum(-1, keepdims=True) * pg[:, None]
    eid = gidx[:, None].astype(jnp.int32) * EXPERTS_PER_GROUP + i2.astype(jnp.int32)
    return eid, wts


def _dispatch(eid, n_experts):
    T = eid.shape[0]
    TK = T * TOP_K
    NR = (TK // MOE_BLOCK + n_experts + 1) * MOE_BLOCK
    e_flat = eid.reshape(TK)
    onehot = (e_flat[:, None] == jnp.arange(n_experts, dtype=jnp.int32)[None, :]).astype(jnp.int32)
    csum = jnp.cumsum(onehot, axis=0)
    counts = csum[-1]
    rank = jnp.sum(csum * onehot, axis=1) - 1
    padded = (counts + MOE_BLOCK - 1) // MOE_BLOCK * MOE_BLOCK
    pad_end = jnp.cumsum(padded)
    pad_start = pad_end - padded
    dest = pad_start[e_flat] + rank
    tok_flat = jnp.repeat(jnp.arange(T, dtype=jnp.int32), TOP_K)
    row_tok = jnp.zeros((NR,), jnp.int32).at[dest].set(tok_flat)
    cap = MOE_ITEM_BLOCKS * MOE_BLOCK
    n_it = (counts + cap - 1) // cap
    it_end = jnp.cumsum(n_it)
    it_start = it_end - n_it
    NI = TK // cap + n_experts
    idx = jnp.arange(NI, dtype=jnp.int32)
    item_e = jnp.minimum(jnp.searchsorted(it_end, idx, side='right'), n_experts - 1).astype(jnp.int32)
    k = idx - it_start[item_e]
    live = idx < it_end[-1]
    item_row0 = jnp.where(live, pad_start[item_e] + k * cap, 0).astype(jnp.int32)
    item_nblk = jnp.where(live, jnp.clip(padded[item_e] // MOE_BLOCK - k * MOE_ITEM_BLOCKS,
                                         0, MOE_ITEM_BLOCKS), 0).astype(jnp.int32)
    n_items = jnp.stack([it_end[-1], pad_end[-1] // MOE_BLOCK]).astype(jnp.int32)
    return dest.reshape(T, TOP_K), row_tok, item_e, item_row0, item_nblk, n_items


def kernel(x, ln_in_g, ln_in_b, w_in, conv_m_w, conv_m_b, w_q, w_k, w_v, w_if, b_if, mlstm_norm_w, mlstm_skip, conv_s_w, conv_s_b, dt_bias, a_log, d_skip, ssd_norm_w, w_out, ln1_g, ln1_b, w_router_coarse, b_router_coarse, w_router_fine, b_router_fine, w_gate_e, w_up_e, w_down_e, ln2_g, ln2_b):
    bsz, seq, D = x.shape
    T = bsz * seq
    nc = seq // CHUNK
    DM = conv_m_w.shape[-1]
    DS = ssd_norm_w.shape[-1]
    n_heads_s = DS // SSD_HEAD_DIM
    n_experts = w_gate_e.shape[1]
    assert w_in.shape[0] == DEPTH
    l = 0
    x2 = x.reshape(T, D)
    row = lambda a: a.reshape(1, -1)

    n_main = w_in.shape[-1] - n_heads_s
    w_main = w_in[l, :, :n_main].astype(BF16)
    w_dt = _pad_lanes(w_in[l, :, n_main:])
    proj, dt_raw = _ln_proj(x2, row(ln_in_g), row(ln_in_b), w_main, w_dt, tm=1024, tn=1024)

    wif = _pad_lanes(w_if[l]).astype(BF16).reshape(3, DM, LANES)
    y_m = _mlstm(proj, bsz, nc, conv_m_w[l], row(conv_m_b[l]),
                 _block_diag_tiles(w_q[l]), _block_diag_tiles(w_k[l]), _block_diag_tiles(w_v[l]),
                 wif, _pad_lanes(row(b_if[l])), row(mlstm_norm_w[l]), row(mlstm_skip[l]))

    y_s = _ssd(proj, dt_raw, bsz, nc, DM, conv_s_w[l], row(conv_s_b[l]),
               _pad_lanes(row(dt_bias[l])), _pad_lanes(row(a_log[l])),
               row(jnp.repeat(d_skip[l], SSD_HEAD_DIM)), row(ssd_norm_w[l]))

    w_o = w_out[l].astype(BF16)
    w_r = _pad_lanes(jnp.concatenate([w_router_coarse[l], w_router_fine[l]], axis=1))
    h1, logits = _out_ln1(y_m, y_s, x2, row(ln_in_g), row(ln_in_b), w_o[:DM], w_o[DM:],
                          row(ln1_g[l]), row(ln1_b[l]), w_r, tm=256)

    eid, wts = _route(logits, b_router_coarse[l], b_router_fine[l])
    dest, row_tok, item_e, item_row0, item_nblk, n_items = _dispatch(eid, n_experts)
    x_sorted = jnp.take(h1, row_tok, axis=0, mode='clip')
    yb = _moe_experts(item_e, item_row0, item_nblk, n_items, x_sorted,
                      w_gate_e[l], w_up_e[l], w_down_e[l])

    m0 = jnp.take(yb, dest[:, 0], axis=0, mode='clip')
    m1 = jnp.take(yb, dest[:, 1], axis=0, mode='clip')
    out = _combine_ln2(h1, m0, m1, wts, row(ln2_g[l]), row(ln2_b[l]), tm=512)
    return out.reshape(bsz, seq, D)
```

```python
import functools
import math

import jax
import jax.numpy as jnp
from jax import lax
from jax.experimental import pallas as pl
from jax.experimental.pallas import tpu as pltpu

F32 = jnp.float32
BF16 = jnp.bfloat16

LANES = 128
MXU_DIM = 256
VMEM_LIMIT = 56 * 1024 * 1024

MLSTM_HEADS = 4
QKV_BLOCK = 4
CHUNK = 128
SSD_HEAD_DIM = 64
SSD_GROUPS = 4
SSD_STATE = 128
CONV_WIDTH = 4
MOE_GROUPS = 8
EXPERTS_PER_GROUP = 8
TOP_K = 2
MOE_BLOCK = 128
MOE_ITEM_BLOCKS = 4
MOE_FF_TILE = 256
DEPTH = 1
DEEPNORM_ALPHA = (2 * DEPTH) ** 0.25
LN_EPS = 1e-5
HALO = 8
LN_ROWS = 256


def _params(semantics):
    return pltpu.CompilerParams(dimension_semantics=semantics, vmem_limit_bytes=VMEM_LIMIT)


def _bdot(a, b):
    return jnp.dot(a.astype(BF16), b.astype(BF16), preferred_element_type=F32)


def _bdot_nt(a, b):
    return lax.dot_general(a.astype(BF16), b.astype(BF16), (((1,), (1,)), ((), ())),
                           preferred_element_type=F32)


def _split2(a):
    hi = a.astype(BF16)
    lo = (a - hi.astype(F32)).astype(BF16)
    return hi, lo


def _dot_exact_rhs(a, b01):
    hi, lo = _split2(a)
    return (jnp.dot(hi, b01, preferred_element_type=F32)
            + jnp.dot(lo, b01, preferred_element_type=F32))


def _dot_exact_lhs(a01, b):
    hi, lo = _split2(b)
    return (jnp.dot(a01, hi, preferred_element_type=F32)
            + jnp.dot(a01, lo, preferred_element_type=F32))


def _dot3(a, b):
    ah, al = _split2(a)
    bh, bl = _split2(b)
    return (jnp.dot(ah, bh, preferred_element_type=F32)
            + jnp.dot(al, bh, preferred_element_type=F32)
            + jnp.dot(ah, bl, preferred_element_type=F32))


def _sigmoid(x):
    return 1.0 / (1.0 + jnp.exp(-x))


def _silu(x):
    return x * _sigmoid(x)


def _softplus(x):
    return jnp.maximum(x, 0.0) + jnp.log1p(jnp.exp(-jnp.abs(x)))


def _layer_norm(x, g, b):
    mu = jnp.mean(x, axis=-1, keepdims=True)
    xc = x - mu
    var = jnp.mean(xc * xc, axis=-1, keepdims=True)
    return xc * lax.rsqrt(var + LN_EPS) * g + b


def _lower_tri(n, strict=False):
    r = lax.broadcasted_iota(jnp.int32, (n, n), 0)
    c = lax.broadcasted_iota(jnp.int32, (n, n), 1)
    return (r > c) if strict else (r >= c)


def _causal_conv(ext_ref, cur, halo_ref, w, b, first):
    L = cur.shape[0]

    @pl.when(first)
    def _():
        halo_ref[...] = jnp.zeros_like(halo_ref)

    ext_ref[0:HALO, :] = halo_ref[...]
    ext_ref[HALO:HALO + L, :] = cur
    halo_ref[...] = cur[L - HALO:, :]
    acc = b + w[CONV_WIDTH - 1:CONV_WIDTH, :] * cur
    for k in range(CONV_WIDTH - 1):
        off = HALO - (CONV_WIDTH - 1) + k
        acc = acc + w[k:k + 1, :] * ext_ref[off:off + L, :]
    return acc


def _ln_proj_kernel(x_ref, g_ref, b_ref, w_ref, wdt_ref, o_ref, dt_ref, hn_ref):
    j = pl.program_id(1)

    @pl.when(j == 0)
    def _():
        def chunk(r, carry):
            rows = pl.ds(pl.multiple_of(r * LN_ROWS, LN_ROWS), LN_ROWS)
            h = _layer_norm(x_ref[rows, :], g_ref[...], b_ref[...])
            hn_ref[rows, :] = h.astype(BF16)
            dt_ref[rows, :] = _dot3(h, wdt_ref[...])
            return carry

        lax.fori_loop(0, x_ref.shape[0] // LN_ROWS, chunk, 0)

    o_ref[...] = jnp.dot(hn_ref[...], w_ref[0].astype(BF16), preferred_element_type=F32)


def _ln_proj(x2, g, b, w_in, layer, n_main, w_dt, tm, tn):
    T, D = x2.shape
    N = n_main
    return pl.pallas_call(
        _ln_proj_kernel,
        out_shape=(jax.ShapeDtypeStruct((T, N), F32), jax.ShapeDtypeStruct((T, LANES), F32)),
        grid=(T // tm, N // tn),
        in_specs=[
            pl.BlockSpec((tm, D), lambda i, j: (i, 0)),
            pl.BlockSpec((1, D), lambda i, j: (0, 0)),
            pl.BlockSpec((1, D), lambda i, j: (0, 0)),
            pl.BlockSpec((1, D, tn), lambda i, j: (layer, 0, j)),
            pl.BlockSpec((D, LANES), lambda i, j: (0, 0)),
        ],
        out_specs=(pl.BlockSpec((tm, tn), lambda i, j: (i, j)),
                   pl.BlockSpec((tm, LANES), lambda i, j: (i, 0))),
        scratch_shapes=[pltpu.VMEM((tm, D), BF16)],
        compiler_params=_params(("parallel", "arbitrary")),
        name="ln_proj",
    )(x2, g, b, w_in, w_dt)


def _mlstm_kernel(xm_ref, og_ref, cw_ref, cb_ref, wq_ref, wk_ref, wv_ref, wif_ref, bif_ref,
                  nw_ref, skip_ref, y_ref, ext_ref, halo_ref, ct_ref, n_ref, m_ref):
    L, DM = xm_ref.shape
    H = MLSTM_HEADS
    Dh = DM // H
    first = pl.program_id(1) == 0

    @pl.when(first)
    def _():
        ct_ref[...] = jnp.zeros_like(ct_ref)
        n_ref[...] = jnp.zeros_like(n_ref)
        m_ref[...] = jnp.zeros_like(m_ref)

    xm = xm_ref[...]
    xc = _silu(_causal_conv(ext_ref, xm, halo_ref, cw_ref[...], cb_ref[...], first))
    xcb = xc.astype(BF16)
    xmb = xm.astype(BF16)
    nblk = DM // MXU_DIM
    q = jnp.concatenate([jnp.dot(xcb[:, i * MXU_DIM:(i + 1) * MXU_DIM], wq_ref[i],
                                 preferred_element_type=F32) for i in range(nblk)], axis=1)
    k = jnp.concatenate([jnp.dot(xcb[:, i * MXU_DIM:(i + 1) * MXU_DIM], wk_ref[i],
                                 preferred_element_type=F32) for i in range(nblk)], axis=1)
    v = jnp.concatenate([jnp.dot(xmb[:, i * MXU_DIM:(i + 1) * MXU_DIM], wv_ref[i],
                                 preferred_element_type=F32) for i in range(nblk)], axis=1)
    qb = q.astype(BF16)
    vb = v.astype(BF16)
    gates = (jnp.dot(qb, wif_ref[0], preferred_element_type=F32)
             + jnp.dot(k.astype(BF16), wif_ref[1], preferred_element_type=F32)
             + jnp.dot(vb, wif_ref[2], preferred_element_type=F32)) + bif_ref[...]
    logf = jnp.minimum(gates, 0.0) - jnp.log1p(jnp.exp(-jnp.abs(gates)))
    tri = _lower_tri(L).astype(BF16)
    bcum = _dot_exact_lhs(tri, logf)
    gates_t = gates.T
    bcum_t = bcum.T
    causal = _lower_tri(L)
    ks = k * (1.0 / math.sqrt(Dh))

    outs = []
    for h in range(H):
        sl = slice(h * Dh, (h + 1) * Dh)
        qh = qb[:, sl]
        kh = ks[:, sl]
        vh = vb[:, sl]
        bc = bcum[:, H + h:H + h + 1]
        br = bcum_t[H + h:H + h + 1, :]
        ic = gates[:, h:h + 1]
        ir = gates_t[h:h + 1, :]
        m_prev = m_ref[h:h + 1, 0:1]
        dlog = jnp.where(causal, bc - br + ir, -jnp.inf)
        inter = bc + m_prev
        mt = jnp.maximum(inter, jnp.max(dlog, axis=-1, keepdims=True))
        w_intra = jnp.exp(dlog - mt)
        w_inter = jnp.exp(inter - mt)
        sc = _bdot_nt(qh, kh) * w_intra
        ct = ct_ref[h]
        num = _bdot(sc, vh) + w_inter * _bdot(qh, ct)
        qn = jnp.sum(q[:, sl] * n_ref[h:h + 1, :], axis=-1, keepdims=True)
        den = jnp.sum(sc, axis=-1, keepdims=True) + w_inter * qn
        hb = num / jnp.maximum(jnp.abs(den), jnp.exp(-mt))
        mu = jnp.mean(hb, axis=-1, keepdims=True)
        hc = hb - mu
        var = jnp.mean(hc * hc, axis=-1, keepdims=True)
        outs.append(hc * lax.rsqrt(var + LN_EPS))
        bl = bc[L - 1:L, :]
        wlog = bl - bc + ic
        m_new = jnp.maximum(bl + m_prev, jnp.max(wlog, axis=0, keepdims=True))
        ws = jnp.exp(wlog - m_new)
        decay = jnp.exp(bl + m_prev - m_new)
        kw = kh * ws
        ct_ref[h] = decay * ct + lax.dot_general(
            kw.astype(BF16), vh, (((0,), (0,)), ((), ())), preferred_element_type=F32)
        n_ref[h:h + 1, :] = decay * n_ref[h:h + 1, :] + jnp.sum(kw, axis=0, keepdims=True)
        m_ref[h:h + 1, :] = jnp.broadcast_to(m_new, (1, m_ref.shape[1]))

    hn = jnp.concatenate(outs, axis=1) * nw_ref[...] + skip_ref[...] * xc
    y_ref[...] = (_sigmoid(og_ref[...]) * hn).astype(y_ref.dtype)


def _mlstm(proj, bsz, nc, conv_w, conv_b, wq_bd, wk_bd, wv_bd, wif, bif, norm_w, skip):
    L = CHUNK
    DM = conv_w.shape[1]
    Dh = DM // MLSTM_HEADS
    T = proj.shape[0]
    const2 = lambda b, c: (0, 0)
    const3 = lambda b, c: (0, 0, 0)
    return pl.pallas_call(
        _mlstm_kernel,
        out_shape=jax.ShapeDtypeStruct((T, DM), BF16),
        grid=(bsz, nc),
        in_specs=[
            pl.BlockSpec((L, DM), lambda b, c: (b * nc + c, 0)),
            pl.BlockSpec((L, DM), lambda b, c: (b * nc + c, 1)),
            pl.BlockSpec(conv_w.shape, const2),
            pl.BlockSpec(conv_b.shape, const2),
            pl.BlockSpec(wq_bd.shape, const3),
            pl.BlockSpec(wk_bd.shape, const3),
            pl.BlockSpec(wv_bd.shape, const3),
            pl.BlockSpec(wif.shape, const3),
            pl.BlockSpec(bif.shape, const2),
            pl.BlockSpec(norm_w.shape, const2),
            pl.BlockSpec(skip.shape, const2),
        ],
        out_specs=pl.BlockSpec((L, DM), lambda b, c: (b * nc + c, 0)),
        scratch_shapes=[
            pltpu.VMEM((HALO + L, DM), F32),
            pltpu.VMEM((HALO, DM), F32),
            pltpu.VMEM((MLSTM_HEADS, Dh, Dh), F32),
            pltpu.VMEM((8, Dh), F32),
            pltpu.VMEM((8, LANES), F32),
        ],
        compiler_params=_params(("parallel", "arbitrary")),
        name="mlstm",
    )(proj, proj, conv_w, conv_b, wq_bd, wk_bd, wv_bd, wif, bif, norm_w, skip)


def _ssd_kernel(z_ref, xs_ref, bc_ref, dt_ref, cw_ref, cb_ref, dtb_ref, alog_ref, dskip_ref,
                nw_ref, y_ref, ext_ref, halo_ref, st_ref):
    L, DS = xs_ref.shape
    G, N, P = SSD_GROUPS, SSD_STATE, SSD_HEAD_DIM
    GN = G * N
    DG = DS // G
    first = pl.program_id(1) == 0

    @pl.when(first)
    def _():
        st_ref[...] = jnp.zeros_like(st_ref)

    cur = jnp.concatenate([xs_ref[...], bc_ref[...]], axis=1)
    xbc = _silu(_causal_conv(ext_ref, cur, halo_ref, cw_ref[...], cb_ref[...], first))
    xh = xbc[:, :DS]
    bm = xbc[:, DS:DS + GN]
    cm = xbc[:, DS + GN:]

    dt = _softplus(dt_ref[...] + dtb_ref[...])
    a = -jnp.exp(alog_ref[...])
    adt = dt * a
    tri = _lower_tri(L).astype(BF16)
    a_cs = _dot_exact_lhs(tri, adt)
    a_cs_t = a_cs.T
    ea = jnp.exp(a_cs)
    a_last = a_cs[L - 1:L, :]
    dstate = jnp.exp(a_last - a_cs)
    er = lax.broadcasted_iota(jnp.int32, (LANES, DS), 0)
    ec = lax.broadcasted_iota(jnp.int32, (LANES, DS), 1)
    expand = (ec // P == er).astype(BF16)
    stacked = jnp.concatenate([dt, ea, dstate], axis=0)
    ex = _dot_exact_rhs(stacked, expand)
    dt_x, ea_x, ds_x = ex[:L], ex[L:2 * L], ex[2 * L:]
    xd = xh * dt_x
    xdd = xd * ds_x
    xdb = xd.astype(BF16)
    causal = _lower_tri(L)
    lane = lax.broadcasted_iota(jnp.int32, (L, LANES), 1)
    lo_half = lane < P

    y_parts = []
    heads_per_group = DG // P
    for g in range(G):
        gs = slice(g * N, (g + 1) * N)
        cg = cm[:, gs].astype(BF16)
        bg = bm[:, gs].astype(BF16)
        cbg = _bdot_nt(cg, bg)
        cols = slice(g * DG, (g + 1) * DG)
        st = st_ref[:, cols]
        y_off = jnp.dot(cg, st.astype(BF16), preferred_element_type=F32) * ea_x[:, cols]
        yd = []
        for pr in range(heads_per_group // 2):
            h0 = g * heads_per_group + 2 * pr
            sc = []
            for h in (h0, h0 + 1):
                diff = a_cs[:, h:h + 1] - a_cs_t[h:h + 1, :]
                sc.append((cbg * jnp.exp(jnp.where(causal, diff, -jnp.inf))).astype(BF16))
            lhs = jnp.concatenate(sc, axis=1)
            xp = xdb[:, h0 * P:(h0 + 2) * P]
            zero = jnp.zeros_like(xp)
            rhs = jnp.concatenate([jnp.where(lo_half, xp, zero),
                                   jnp.where(lo_half, zero, xp)], axis=0)
            yd.append(jnp.dot(lhs, rhs, preferred_element_type=F32))
        y_parts.append(jnp.concatenate(yd, axis=1) + y_off)
        st_ref[:, cols] = ea_x[L - 1:L, cols] * st + lax.dot_general(
            bg, xdd[:, cols].astype(BF16), (((0,), (0,)), ((), ())), preferred_element_type=F32)

    y = jnp.concatenate(y_parts, axis=1) + dskip_ref[...] * xh
    y = y * _silu(z_ref[...])
    normed = []
    for g in range(G):
        yg = y[:, g * DG:(g + 1) * DG]
        normed.append(yg * lax.rsqrt(jnp.mean(yg * yg, axis=-1, keepdims=True) + LN_EPS))
    y_ref[...] = (jnp.concatenate(normed, axis=1) * nw_ref[...]).astype(y_ref.dtype)


def _ssd(proj, dt_raw, bsz, nc, d_mlstm, conv_w, conv_b, dt_bias, a_log, dskip_x, norm_w):
    L = CHUNK
    DS = norm_w.shape[1]
    GN2 = 2 * SSD_GROUPS * SSD_STATE
    T = proj.shape[0]
    const2 = lambda b, c: (0, 0)
    z_blk = 2 * d_mlstm // DS
    bc_blk = (2 * d_mlstm + 2 * DS) // GN2
    return pl.pallas_call(
        _ssd_kernel,
        out_shape=jax.ShapeDtypeStruct((T, DS), BF16),
        grid=(bsz, nc),
        in_specs=[
            pl.BlockSpec((L, DS), lambda b, c: (b * nc + c, z_blk)),
            pl.BlockSpec((L, DS), lambda b, c: (b * nc + c, z_blk + 1)),
            pl.BlockSpec((L, GN2), lambda b, c: (b * nc + c, bc_blk)),
            pl.BlockSpec((L, LANES), lambda b, c: (b * nc + c, 0)),
            pl.BlockSpec(conv_w.shape, const2),
            pl.BlockSpec(conv_b.shape, const2),
            pl.BlockSpec(dt_bias.shape, const2),
            pl.BlockSpec(a_log.shape, const2),
            pl.BlockSpec(dskip_x.shape, const2),
            pl.BlockSpec(norm_w.shape, const2),
        ],
        out_specs=pl.BlockSpec((L, DS), lambda b, c: (b * nc + c, 0)),
        scratch_shapes=[
            pltpu.VMEM((HALO + L, DS + GN2), F32),
            pltpu.VMEM((HALO, DS + GN2), F32),
            pltpu.VMEM((SSD_STATE, DS), F32),
        ],
        compiler_params=_params(("parallel", "arbitrary")),
        name="ssd",
    )(proj, proj, proj, dt_raw, conv_w, conv_b, dt_bias, a_log, dskip_x, norm_w)


def _out_ln1_kernel(ym_ref, ys_ref, x_ref, g0_ref, b0_ref, wm_ref, ws_ref, g1_ref, b1_ref,
                    wr_ref, h1_ref, lg_ref):
    h0 = _layer_norm(x_ref[...], g0_ref[...], b0_ref[...])
    mix = (jnp.dot(ym_ref[...], wm_ref[...], preferred_element_type=F32)
           + jnp.dot(ys_ref[...], ws_ref[...], preferred_element_type=F32))
    h1 = _layer_norm(DEEPNORM_ALPHA * h0 + mix, g1_ref[...], b1_ref[...])
    h1_ref[...] = h1
    lg_ref[...] = _dot3(h1, wr_ref[...])


def _out_ln1(ym, ys, x2, g0, b0, w_o, g1, b1, w_r, tm):
    T, D = x2.shape
    DM = ym.shape[1]
    DS = ys.shape[1]
    assert DM % DS == 0
    row = lambda i: (i, 0)
    const = lambda i: (0, 0)
    return pl.pallas_call(
        _out_ln1_kernel,
        out_shape=(jax.ShapeDtypeStruct((T, D), F32), jax.ShapeDtypeStruct((T, LANES), F32)),
        grid=(T // tm,),
        in_specs=[
            pl.BlockSpec((tm, DM), row), pl.BlockSpec((tm, DS), row), pl.BlockSpec((tm, D), row),
            pl.BlockSpec((1, D), const), pl.BlockSpec((1, D), const),
            pl.BlockSpec((DM, D), const), pl.BlockSpec((DS, D), lambda i: (DM // DS, 0)),
            pl.BlockSpec((1, D), const), pl.BlockSpec((1, D), const),
            pl.BlockSpec((D, LANES), const),
        ],
        out_specs=(pl.BlockSpec((tm, D), row), pl.BlockSpec((tm, LANES), row)),
        compiler_params=_params(("parallel",)),
        name="out_ln1",
    )(ym, ys, x2, g0, b0, w_o, w_o, g1, b1, w_r)


def _moe_kernel(ie_ref, ir_ref, inb_ref, nit_ref, x_hbm, wg_ref, wu_ref, wd_ref, o_hbm,
                xbuf, acc, wgb, wub, wdb, xsem, osem):
    i = pl.program_id(0)
    f = pl.program_id(1)
    nf = pl.num_programs(1)
    n_items = nit_ref[0]
    live = i < n_items
    slot = i % 2

    def x_copy(item, s, j):
        rows = pl.ds(pl.multiple_of(ir_ref[item] + j * MOE_BLOCK, MOE_BLOCK), MOE_BLOCK)
        return pltpu.make_async_copy(x_hbm.at[rows], xbuf.at[s, pl.ds(j * MOE_BLOCK, MOE_BLOCK)],
                                     xsem.at[s])

    def o_copy(item, s, j):
        rows = pl.ds(pl.multiple_of(ir_ref[item] + j * MOE_BLOCK, MOE_BLOCK), MOE_BLOCK)
        return pltpu.make_async_copy(acc.at[s, pl.ds(j * MOE_BLOCK, MOE_BLOCK)], o_hbm.at[rows],
                                     osem.at[s])

    def for_blocks(item, fn):
        n = inb_ref[item]
        for j in range(MOE_ITEM_BLOCKS):
            pl.when(j < n)(functools.partial(fn, j))

    @pl.when(jnp.logical_and(live, f == 0))
    def _():
        @pl.when(i == 0)
        def _():
            for_blocks(i, lambda j: x_copy(i, slot, j).start())

        @pl.when(i + 1 < n_items)
        def _():
            for_blocks(i + 1, lambda j: x_copy(i + 1, 1 - slot, j).start())

        for_blocks(i, lambda j: x_copy(i, slot, j).wait())

        @pl.when(i >= 2)
        def _():
            for_blocks(i - 2, lambda j: o_copy(i - 2, slot, j).wait())

    @pl.when(live)
    def _():
        wgb[...] = wg_ref[0].astype(BF16)
        wub[...] = wu_ref[0].astype(BF16)
        wdb[...] = wd_ref[0].astype(BF16)

        def compute(n_rows):
            rows = pl.ds(0, n_rows)
            xb = xbuf[slot, rows, :].astype(BF16)
            a = jnp.dot(xb, wgb[...], preferred_element_type=F32)
            u = jnp.dot(xb, wub[...], preferred_element_type=F32)
            part = jnp.dot((_silu(a) * u).astype(BF16), wdb[...], preferred_element_type=F32)

            @pl.when(f == 0)
            def _():
                acc[slot, rows, :] = part

            @pl.when(f > 0)
            def _():
                acc[slot, rows, :] += part

        for n in range(1, MOE_ITEM_BLOCKS + 1):
            pl.when(inb_ref[i] == n)(functools.partial(compute, n * MOE_BLOCK))

    @pl.when(jnp.logical_and(live, f == nf - 1))
    def _():
        for_blocks(i, lambda j: o_copy(i, slot, j).start())

        @pl.when(i == n_items - 1)
        def _():
            for_blocks(i, lambda j: o_copy(i, slot, j).wait())

            @pl.when(i >= 1)
            def _():
                for_blocks(i - 1, lambda j: o_copy(i - 1, 1 - slot, j).wait())

            zrows = pl.ds(0, MOE_BLOCK)
            xbuf[slot, zrows, :] = jnp.zeros((MOE_BLOCK, xbuf.shape[2]), xbuf.dtype)

            def z_copy(b):
                rows = pl.ds(pl.multiple_of(b * MOE_BLOCK, MOE_BLOCK), MOE_BLOCK)
                return pltpu.make_async_copy(xbuf.at[slot, zrows], o_hbm.at[rows], osem.at[slot])

            first, last = nit_ref[1], o_hbm.shape[0] // MOE_BLOCK
            lax.fori_loop(first, last, lambda b, c: (z_copy(b).start(), c)[1], 0)
            lax.fori_loop(first, last, lambda b, c: (z_copy(b).wait(), c)[1], 0)


def _moe_experts(item_e, item_row0, item_nblk, n_items, x_sorted, w_gate, w_up, w_down):
    NR, D = x_sorted.shape
    NI = item_e.shape[0]
    FF = w_gate.shape[2]
    tf = MOE_FF_TILE
    nf = FF // tf
    cap = MOE_ITEM_BLOCKS * MOE_BLOCK

    def expert(i, ie, nit):
        return ie[jnp.minimum(i, nit[0] - 1)]

    def ff(i, f, nit):
        return jnp.where(i < nit[0], f, nf - 1)

    return pl.pallas_call(
        _moe_kernel,
        out_shape=jax.ShapeDtypeStruct((NR, D), F32),
        grid_spec=pltpu.PrefetchScalarGridSpec(
            num_scalar_prefetch=4,
            grid=(NI, nf),
            in_specs=[
                pl.BlockSpec(memory_space=pl.ANY),
                pl.BlockSpec((1, D, tf),
                             lambda i, f, ie, ir, inb, nit: (expert(i, ie, nit), 0, ff(i, f, nit))),
                pl.BlockSpec((1, D, tf),
                             lambda i, f, ie, ir, inb, nit: (expert(i, ie, nit), 0, ff(i, f, nit))),
                pl.BlockSpec((1, tf, D),
                             lambda i, f, ie, ir, inb, nit: (expert(i, ie, nit), ff(i, f, nit), 0)),
            ],
            out_specs=pl.BlockSpec(memory_space=pl.ANY),
            scratch_shapes=[
                pltpu.VMEM((2, cap, D), F32),
                pltpu.VMEM((2, cap, D), F32),
                pltpu.VMEM((D, tf), BF16),
                pltpu.VMEM((D, tf), BF16),
                pltpu.VMEM((tf, D), BF16),
                pltpu.SemaphoreType.DMA((2,)),
                pltpu.SemaphoreType.DMA((2,)),
            ],
        ),
        compiler_params=_params(("arbitrary", "arbitrary")),
        name="moe_experts",
    )(item_e, item_row0, item_nblk, n_items, x_sorted, w_gate, w_up, w_down)


def _combine_ln2_kernel(h_ref, m0_ref, m1_ref, w_ref, g_ref, b_ref, o_ref):
    w = w_ref[...]
    moe = m0_ref[...] * w[:, 0:1] + m1_ref[...] * w[:, 1:2]
    o_ref[...] = _layer_norm(DEEPNORM_ALPHA * h_ref[...] + moe, g_ref[...], b_ref[...])


def _combine_ln2(h1, m0, m1, wts, g, b, tm):
    T, D = h1.shape
    row = lambda i: (i, 0)
    const = lambda i: (0, 0)
    return pl.pallas_call(
        _combine_ln2_kernel,
        out_shape=jax.ShapeDtypeStruct((T, D), F32),
        grid=(T // tm,),
        in_specs=[pl.BlockSpec((tm, D), row), pl.BlockSpec((tm, D), row), pl.BlockSpec((tm, D), row),
                  pl.BlockSpec((tm, TOP_K), row),
                  pl.BlockSpec((1, D), const), pl.BlockSpec((1, D), const)],
        out_specs=pl.BlockSpec((tm, D), row),
        compiler_params=_params(("parallel",)),
        name="combine_ln2",
    )(h1, m0, m1, wts, g, b)


def _block_diag_tiles(w):
    nb = w.shape[0]
    n_tiles = nb * QKV_BLOCK // MXU_DIM
    rows = w.reshape(n_tiles, MXU_DIM, QKV_BLOCK)
    tiled = jnp.tile(rows, (1, 1, MXU_DIM // QKV_BLOCK))
    r = jnp.arange(MXU_DIM)[:, None] // QKV_BLOCK
    c = jnp.arange(MXU_DIM)[None, :] // QKV_BLOCK
    return jnp.where(r == c, tiled, 0.0).astype(BF16)


def _pad_lanes(a, width=LANES):
    return jnp.pad(a, [(0, 0)] * (a.ndim - 1) + [(0, width - a.shape[-1])])


def _route(logits, b_rc, b_rf):
    T = logits.shape[0]
    pc = jax.nn.softmax(logits[:, :MOE_GROUPS] + b_rc, axis=-1)
    gidx = jnp.argmax(pc, axis=-1)
    pg = jnp.max(pc, axis=-1)
    lf = (logits[:, MOE_GROUPS:MOE_GROUPS + MOE_GROUPS * EXPERTS_PER_GROUP] + b_rf)
    lf = lf.reshape(T, MOE_GROUPS, EXPERTS_PER_GROUP)
    sel = jnp.take_along_axis(lf, gidx[:, None, None], axis=1)[:, 0]
    pf = jax.nn.softmax(sel, axis=-1)
    v2, i2 = lax.top_k(pf, TOP_K)
    wts = v2 / v2.sum(-1, keepdims=True) * pg[:, None]
    eid = gidx[:, None].astype(jnp.int32) * EXPERTS_PER_GROUP + i2.astype(jnp.int32)
    return eid, wts


def _dispatch(eid, n_experts):
    T = eid.shape[0]
    TK = T * TOP_K
    NR = (TK // MOE_BLOCK + n_experts + 1) * MOE_BLOCK
    e_flat = eid.reshape(TK)
    onehot = (e_flat[:, None] == jnp.arange(n_experts, dtype=jnp.int32)[None, :]).astype(jnp.int32)
    csum = jnp.cumsum(onehot, axis=0)
    counts = csum[-1]
    rank = jnp.sum(csum * onehot, axis=1) - 1
    padded = (counts + MOE_BLOCK - 1) // MOE_BLOCK * MOE_BLOCK
    pad_end = jnp.cumsum(padded)
    pad_start = pad_end - padded
    dest = pad_start[e_flat] + rank
    tok_flat = jnp.repeat(jnp.arange(T, dtype=jnp.int32), TOP_K)
    row_tok = (jnp.arange(NR, dtype=jnp.int32) % T).at[dest].set(tok_flat)
    cap = MOE_ITEM_BLOCKS * MOE_BLOCK
    n_it = (counts + cap - 1) // cap
    it_end = jnp.cumsum(n_it)
    it_start = it_end - n_it
    NI = TK // cap + n_experts
    idx = jnp.arange(NI, dtype=jnp.int32)
    item_e = jnp.minimum(jnp.searchsorted(it_end, idx, side='right'), n_experts - 1).astype(jnp.int32)
    k = idx - it_start[item_e]
    live = idx < it_end[-1]
    item_row0 = jnp.where(live, pad_start[item_e] + k * cap, 0).astype(jnp.int32)
    item_nblk = jnp.where(live, jnp.clip(padded[item_e] // MOE_BLOCK - k * MOE_ITEM_BLOCKS,
                                         0, MOE_ITEM_BLOCKS), 0).astype(jnp.int32)
    n_items = jnp.stack([it_end[-1], pad_end[-1] // MOE_BLOCK]).astype(jnp.int32)
    return dest.reshape(T, TOP_K), row_tok, item_e, item_row0, item_nblk, n_items


def kernel(x, ln_in_g, ln_in_b, w_in, conv_m_w, conv_m_b, w_q, w_k, w_v, w_if, b_if, mlstm_norm_w, mlstm_skip, conv_s_w, conv_s_b, dt_bias, a_log, d_skip, ssd_norm_w, w_out, ln1_g, ln1_b, w_router_coarse, b_router_coarse, w_router_fine, b_router_fine, w_gate_e, w_up_e, w_down_e, ln2_g, ln2_b):
    bsz, seq, D = x.shape
    T = bsz * seq
    nc = seq // CHUNK
    DM = conv_m_w.shape[-1]
    DS = ssd_norm_w.shape[-1]
    n_heads_s = DS // SSD_HEAD_DIM
    n_experts = w_gate_e.shape[1]
    assert w_in.shape[0] == DEPTH
    l = 0
    x2 = x.reshape(T, D)
    row = lambda a: a.reshape(1, -1)

    n_main = w_in.shape[-1] - n_heads_s
    w_dt = _pad_lanes(w_in[l, :, n_main:])
    proj, dt_raw = _ln_proj(x2, row(ln_in_g), row(ln_in_b), w_in, l, n_main, w_dt, tm=1024, tn=1024)

    wif = _pad_lanes(w_if[l]).astype(BF16).reshape(3, DM, LANES)
    y_m = _mlstm(proj, bsz, nc, conv_m_w[l], row(conv_m_b[l]),
                 _block_diag_tiles(w_q[l]), _block_diag_tiles(w_k[l]), _block_diag_tiles(w_v[l]),
                 wif, _pad_lanes(row(b_if[l])), row(mlstm_norm_w[l]), row(mlstm_skip[l]))

    y_s = _ssd(proj, dt_raw, bsz, nc, DM, conv_s_w[l], row(conv_s_b[l]),
               _pad_lanes(row(dt_bias[l])), _pad_lanes(row(a_log[l])),
               row(jnp.repeat(d_skip[l], SSD_HEAD_DIM)), row(ssd_norm_w[l]))

    w_o = w_out[l].astype(BF16)
    w_r = _pad_lanes(jnp.concatenate([w_router_coarse[l], w_router_fine[l]], axis=1))
    h1, logits = _out_ln1(y_m, y_s, x2, row(ln_in_g), row(ln_in_b), w_o,
                          row(ln1_g[l]), row(ln1_b[l]), w_r, tm=256)

    eid, wts = _route(logits, b_router_coarse[l], b_router_fine[l])
    dest, row_tok, item_e, item_row0, item_nblk, n_items = _dispatch(eid, n_experts)
    x_sorted = jnp.take(h1, row_tok, axis=0, mode='clip')
    yb = _moe_experts(item_e, item_row0, item_nblk, n_items, x_sorted,
                      w_gate_e[l], w_up_e[l], w_down_e[l])

    m0 = jnp.take(yb, dest[:, 0], axis=0, mode='clip')
    m1 = jnp.take(yb, dest[:, 1], axis=0, mode='clip')
    out = _combine_ln2(h1, m0, m1, wts, row(ln2_g[l]), row(ln2_b[l]), tm=512)
    return out.reshape(bsz, seq, D)
```

```python
import functools
import math

import jax
import jax.numpy as jnp
from jax import lax
from jax.experimental import pallas as pl
from jax.experimental.pallas import tpu as pltpu

F32 = jnp.float32
BF16 = jnp.bfloat16

LANES = 128
MXU_DIM = 256
VMEM_LIMIT = 56 * 1024 * 1024

MLSTM_HEADS = 4
QKV_BLOCK = 4
CHUNK = 128
SSD_HEAD_DIM = 64
SSD_GROUPS = 4
SSD_STATE = 128
CONV_WIDTH = 4
MOE_GROUPS = 8
EXPERTS_PER_GROUP = 8
TOP_K = 2
MOE_BLOCK = 128
MOE_ITEM_BLOCKS = 4
MOE_FF_TILE = 512
DEPTH = 1
DEEPNORM_ALPHA = (2 * DEPTH) ** 0.25
LN_EPS = 1e-5
HALO = 8
LN_ROWS = 256


def _params(semantics):
    return pltpu.CompilerParams(dimension_semantics=semantics, vmem_limit_bytes=VMEM_LIMIT)


def _bdot(a, b):
    return jnp.dot(a.astype(BF16), b.astype(BF16), preferred_element_type=F32)


def _bdot_nt(a, b):
    return lax.dot_general(a.astype(BF16), b.astype(BF16), (((1,), (1,)), ((), ())),
                           preferred_element_type=F32)


def _split2(a):
    hi = a.astype(BF16)
    lo = (a - hi.astype(F32)).astype(BF16)
    return hi, lo


def _dot_exact_rhs(a, b01):
    hi, lo = _split2(a)
    return (jnp.dot(hi, b01, preferred_element_type=F32)
            + jnp.dot(lo, b01, preferred_element_type=F32))


def _dot_exact_lhs(a01, b):
    hi, lo = _split2(b)
    return (jnp.dot(a01, hi, preferred_element_type=F32)
            + jnp.dot(a01, lo, preferred_element_type=F32))


def _dot3(a, b):
    ah, al = _split2(a)
    bh, bl = _split2(b)
    return (jnp.dot(ah, bh, preferred_element_type=F32)
            + jnp.dot(al, bh, preferred_element_type=F32)
            + jnp.dot(ah, bl, preferred_element_type=F32))


def _sigmoid(x):
    return 1.0 / (1.0 + jnp.exp(-x))


def _silu(x):
    return x * _sigmoid(x)


def _softplus(x):
    return jnp.maximum(x, 0.0) + jnp.log1p(jnp.exp(-jnp.abs(x)))


def _layer_norm(x, g, b):
    mu = jnp.mean(x, axis=-1, keepdims=True)
    xc = x - mu
    var = jnp.mean(xc * xc, axis=-1, keepdims=True)
    return xc * lax.rsqrt(var + LN_EPS) * g + b


def _lower_tri(n, strict=False):
    r = lax.broadcasted_iota(jnp.int32, (n, n), 0)
    c = lax.broadcasted_iota(jnp.int32, (n, n), 1)
    return (r > c) if strict else (r >= c)


def _causal_conv(ext_ref, cur, halo_ref, w, b):
    L = cur.shape[0]
    ext_ref[0:HALO, :] = halo_ref[...]
    ext_ref[HALO:HALO + L, :] = cur
    halo_ref[...] = cur[L - HALO:, :]
    acc = b + w[CONV_WIDTH - 1:CONV_WIDTH, :] * cur
    for k in range(CONV_WIDTH - 1):
        off = HALO - (CONV_WIDTH - 1) + k
        acc = acc + w[k:k + 1, :] * ext_ref[off:off + L, :]
    return acc


def _ln_proj_kernel(x_ref, g_ref, b_ref, w_ref, wdt_ref, o_ref, dt_ref, hn_ref):
    j = pl.program_id(1)

    @pl.when(j == 0)
    def _():
        def chunk(r, carry):
            rows = pl.ds(pl.multiple_of(r * LN_ROWS, LN_ROWS), LN_ROWS)
            h = _layer_norm(x_ref[rows, :], g_ref[...], b_ref[...])
            hn_ref[rows, :] = h.astype(BF16)
            dt_ref[rows, :] = _dot3(h, wdt_ref[...])
            return carry

        lax.fori_loop(0, x_ref.shape[0] // LN_ROWS, chunk, 0)

    o_ref[...] = jnp.dot(hn_ref[...], w_ref[...], preferred_element_type=F32)


def _ln_proj(x2, g, b, w_in, n_main, w_dt, tm, tn):
    T, D = x2.shape
    N = n_main
    return pl.pallas_call(
        _ln_proj_kernel,
        out_shape=(jax.ShapeDtypeStruct((T, N), F32), jax.ShapeDtypeStruct((T, LANES), F32)),
        grid=(T // tm, N // tn),
        in_specs=[
            pl.BlockSpec((tm, D), lambda i, j: (i, 0)),
            pl.BlockSpec((1, D), lambda i, j: (0, 0)),
            pl.BlockSpec((1, D), lambda i, j: (0, 0)),
            pl.BlockSpec((D, tn), lambda i, j: (0, j)),
            pl.BlockSpec((D, LANES), lambda i, j: (0, 0)),
        ],
        out_specs=(pl.BlockSpec((tm, tn), lambda i, j: (i, j)),
                   pl.BlockSpec((tm, LANES), lambda i, j: (i, 0))),
        scratch_shapes=[pltpu.VMEM((tm, D), BF16)],
        compiler_params=_params(("parallel", "arbitrary")),
        name="ln_proj",
    )(x2, g, b, w_in, w_dt)


def _mlstm_kernel(xm_ref, og_ref, cw_ref, cb_ref, wq_ref, wk_ref, wv_ref, wif_ref, bif_ref,
                  nw_ref, skip_ref, y_ref, ext_ref, halo_ref, ct_ref, n_ref, m_ref):
    @pl.when(pl.program_id(0) == 0)
    def _():
        halo_ref[...] = jnp.zeros_like(halo_ref)
        ct_ref[...] = jnp.zeros_like(ct_ref)
        n_ref[...] = jnp.zeros_like(n_ref)
        m_ref[...] = jnp.zeros_like(m_ref)

    for b in range(xm_ref.shape[0]):
        _mlstm_chunk(xm_ref.at[b], og_ref.at[b], cw_ref, cb_ref, wq_ref, wk_ref, wv_ref, wif_ref,
                     bif_ref, nw_ref, skip_ref, y_ref.at[b], ext_ref.at[b], halo_ref.at[b],
                     ct_ref.at[b], n_ref.at[b], m_ref.at[b])


def _mlstm_chunk(xm_ref, og_ref, cw_ref, cb_ref, wq_ref, wk_ref, wv_ref, wif_ref, bif_ref,
                 nw_ref, skip_ref, y_ref, ext_ref, halo_ref, ct_ref, n_ref, m_ref):
    L, DM = xm_ref.shape
    H = MLSTM_HEADS
    Dh = DM // H
    xm = xm_ref[...]
    xc = _silu(_causal_conv(ext_ref, xm, halo_ref, cw_ref[...], cb_ref[...]))
    xcb = xc.astype(BF16)
    xmb = xm.astype(BF16)
    nblk = DM // MXU_DIM
    q = jnp.concatenate([jnp.dot(xcb[:, i * MXU_DIM:(i + 1) * MXU_DIM], wq_ref[i],
                                 preferred_element_type=F32) for i in range(nblk)], axis=1)
    k = jnp.concatenate([jnp.dot(xcb[:, i * MXU_DIM:(i + 1) * MXU_DIM], wk_ref[i],
                                 preferred_element_type=F32) for i in range(nblk)], axis=1)
    v = jnp.concatenate([jnp.dot(xmb[:, i * MXU_DIM:(i + 1) * MXU_DIM], wv_ref[i],
                                 preferred_element_type=F32) for i in range(nblk)], axis=1)
    qb = q.astype(BF16)
    vb = v.astype(BF16)
    gates = (jnp.dot(qb, wif_ref[0], preferred_element_type=F32)
             + jnp.dot(k.astype(BF16), wif_ref[1], preferred_element_type=F32)
             + jnp.dot(vb, wif_ref[2], preferred_element_type=F32)) + bif_ref[...]
    logf = jnp.minimum(gates, 0.0) - jnp.log1p(jnp.exp(-jnp.abs(gates)))
    tri = _lower_tri(L).astype(BF16)
    bcum = _dot_exact_lhs(tri, logf)
    gates_t = gates.T
    bcum_t = bcum.T
    causal = _lower_tri(L)
    ks = k * (1.0 / math.sqrt(Dh))

    outs = []
    for h in range(H):
        sl = slice(h * Dh, (h + 1) * Dh)
        qh = qb[:, sl]
        kh = ks[:, sl]
        vh = vb[:, sl]
        bc = bcum[:, H + h:H + h + 1]
        br = bcum_t[H + h:H + h + 1, :]
        ic = gates[:, h:h + 1]
        ir = gates_t[h:h + 1, :]
        m_prev = m_ref[h:h + 1, 0:1]
        dlog = jnp.where(causal, bc - br + ir, -jnp.inf)
        inter = bc + m_prev
        mt = jnp.maximum(inter, jnp.max(dlog, axis=-1, keepdims=True))
        w_intra = jnp.exp(dlog - mt)
        w_inter = jnp.exp(inter - mt)
        sc = _bdot_nt(qh, kh) * w_intra
        ct = ct_ref[h]
        num = _bdot(sc, vh) + w_inter * _bdot(qh, ct)
        qn = jnp.sum(q[:, sl] * n_ref[h:h + 1, :], axis=-1, keepdims=True)
        den = jnp.sum(sc, axis=-1, keepdims=True) + w_inter * qn
        hb = num / jnp.maximum(jnp.abs(den), jnp.exp(-mt))
        mu = jnp.mean(hb, axis=-1, keepdims=True)
        hc = hb - mu
        var = jnp.mean(hc * hc, axis=-1, keepdims=True)
        outs.append(hc * lax.rsqrt(var + LN_EPS))
        bl = bc[L - 1:L, :]
        wlog = bl - bc + ic
        m_new = jnp.maximum(bl + m_prev, jnp.max(wlog, axis=0, keepdims=True))
        ws = jnp.exp(wlog - m_new)
        decay = jnp.exp(bl + m_prev - m_new)
        kw = kh * ws
        ct_ref[h] = decay * ct + lax.dot_general(
            kw.astype(BF16), vh, (((0,), (0,)), ((), ())), preferred_element_type=F32)
        n_ref[h:h + 1, :] = decay * n_ref[h:h + 1, :] + jnp.sum(kw, axis=0, keepdims=True)
        m_ref[h:h + 1, :] = jnp.broadcast_to(m_new, (1, m_ref.shape[1]))

    hn = jnp.concatenate(outs, axis=1) * nw_ref[...] + skip_ref[...] * xc
    y_ref[...] = (_sigmoid(og_ref[...]) * hn).astype(y_ref.dtype)


def _mlstm(proj, bsz, nc, conv_w, conv_b, wq_bd, wk_bd, wv_bd, wif, bif, norm_w, skip):
    L = CHUNK
    DM = conv_w.shape[1]
    Dh = DM // MLSTM_HEADS
    const2 = lambda c: (0, 0)
    const3 = lambda c: (0, 0, 0)
    return pl.pallas_call(
        _mlstm_kernel,
        out_shape=jax.ShapeDtypeStruct((bsz, nc * L, DM), BF16),
        grid=(nc,),
        in_specs=[
            pl.BlockSpec((bsz, L, DM), lambda c: (0, c, 0)),
            pl.BlockSpec((bsz, L, DM), lambda c: (0, c, 1)),
            pl.BlockSpec(conv_w.shape, const2),
            pl.BlockSpec(conv_b.shape, const2),
            pl.BlockSpec(wq_bd.shape, const3),
            pl.BlockSpec(wk_bd.shape, const3),
            pl.BlockSpec(wv_bd.shape, const3),
            pl.BlockSpec(wif.shape, const3),
            pl.BlockSpec(bif.shape, const2),
            pl.BlockSpec(norm_w.shape, const2),
            pl.BlockSpec(skip.shape, const2),
        ],
        out_specs=pl.BlockSpec((bsz, L, DM), lambda c: (0, c, 0)),
        scratch_shapes=[
            pltpu.VMEM((bsz, HALO + L, DM), F32),
            pltpu.VMEM((bsz, HALO, DM), F32),
            pltpu.VMEM((bsz, MLSTM_HEADS, Dh, Dh), F32),
            pltpu.VMEM((bsz, 8, Dh), F32),
            pltpu.VMEM((bsz, 8, LANES), F32),
        ],
        compiler_params=_params(("arbitrary",)),
        name="mlstm",
    )(proj, proj, conv_w, conv_b, wq_bd, wk_bd, wv_bd, wif, bif, norm_w, skip)


def _ssd_kernel(z_ref, xs_ref, bc_ref, dt_ref, cw_ref, cb_ref, dtb_ref, alog_ref, dskip_ref,
                nw_ref, y_ref, ext_ref, halo_ref, st_ref):
    @pl.when(pl.program_id(0) == 0)
    def _():
        halo_ref[...] = jnp.zeros_like(halo_ref)
        st_ref[...] = jnp.zeros_like(st_ref)

    for b in range(xs_ref.shape[0]):
        _ssd_chunk(z_ref.at[b], xs_ref.at[b], bc_ref.at[b], dt_ref.at[b], cw_ref, cb_ref, dtb_ref,
                   alog_ref, dskip_ref, nw_ref, y_ref.at[b], ext_ref.at[b], halo_ref.at[b],
                   st_ref.at[b])


def _ssd_chunk(z_ref, xs_ref, bc_ref, dt_ref, cw_ref, cb_ref, dtb_ref, alog_ref, dskip_ref,
               nw_ref, y_ref, ext_ref, halo_ref, st_ref):
    L, DS = xs_ref.shape
    G, N, P = SSD_GROUPS, SSD_STATE, SSD_HEAD_DIM
    GN = G * N
    DG = DS // G
    cur = jnp.concatenate([xs_ref[...], bc_ref[...]], axis=1)
    xbc = _silu(_causal_conv(ext_ref, cur, halo_ref, cw_ref[...], cb_ref[...]))
    xh = xbc[:, :DS]
    bm = xbc[:, DS:DS + GN]
    cm = xbc[:, DS + GN:]

    dt = _softplus(dt_ref[...] + dtb_ref[...])
    a = -jnp.exp(alog_ref[...])
    adt = dt * a
    tri = _lower_tri(L).astype(BF16)
    a_cs = _dot_exact_lhs(tri, adt)
    a_cs_t = a_cs.T
    ea = jnp.exp(a_cs)
    a_last = a_cs[L - 1:L, :]
    dstate = jnp.exp(a_last - a_cs)
    er = lax.broadcasted_iota(jnp.int32, (LANES, DS), 0)
    ec = lax.broadcasted_iota(jnp.int32, (LANES, DS), 1)
    expand = (ec // P == er).astype(BF16)
    stacked = jnp.concatenate([dt, ea, dstate], axis=0)
    ex = _dot_exact_rhs(stacked, expand)
    dt_x, ea_x, ds_x = ex[:L], ex[L:2 * L], ex[2 * L:]
    xd = xh * dt_x
    xdd = xd * ds_x
    xdb = xd.astype(BF16)
    causal = _lower_tri(L)
    lane = lax.broadcasted_iota(jnp.int32, (L, LANES), 1)
    lo_half = lane < P

    y_parts = []
    heads_per_group = DG // P
    for g in range(G):
        gs = slice(g * N, (g + 1) * N)
        cg = cm[:, gs].astype(BF16)
        bg = bm[:, gs].astype(BF16)
        cbg = _bdot_nt(cg, bg)
        cols = slice(g * DG, (g + 1) * DG)
        st = st_ref[:, cols]
        y_off = jnp.dot(cg, st.astype(BF16), preferred_element_type=F32) * ea_x[:, cols]
        yd = []
        for pr in range(heads_per_group // 2):
            h0 = g * heads_per_group + 2 * pr
            sc = []
            for h in (h0, h0 + 1):
                diff = a_cs[:, h:h + 1] - a_cs_t[h:h + 1, :]
                sc.append((cbg * jnp.exp(jnp.where(causal, diff, -jnp.inf))).astype(BF16))
            lhs = jnp.concatenate(sc, axis=1)
            xp = xdb[:, h0 * P:(h0 + 2) * P]
            zero = jnp.zeros_like(xp)
            rhs = jnp.concatenate([jnp.where(lo_half, xp, zero),
                                   jnp.where(lo_half, zero, xp)], axis=0)
            yd.append(jnp.dot(lhs, rhs, preferred_element_type=F32))
        y_parts.append(jnp.concatenate(yd, axis=1) + y_off)
        st_ref[:, cols] = ea_x[L - 1:L, cols] * st + lax.dot_general(
            bg, xdd[:, cols].astype(BF16), (((0,), (0,)), ((), ())), preferred_element_type=F32)

    y = jnp.concatenate(y_parts, axis=1) + dskip_ref[...] * xh
    y = y * _silu(z_ref[...])
    normed = []
    for g in range(G):
        yg = y[:, g * DG:(g + 1) * DG]
        normed.append(yg * lax.rsqrt(jnp.mean(yg * yg, axis=-1, keepdims=True) + LN_EPS))
    y_ref[...] = (jnp.concatenate(normed, axis=1) * nw_ref[...]).astype(y_ref.dtype)


def _ssd(proj, dt_raw, bsz, nc, d_mlstm, conv_w, conv_b, dt_bias, a_log, dskip_x, norm_w):
    L = CHUNK
    DS = norm_w.shape[1]
    GN2 = 2 * SSD_GROUPS * SSD_STATE
    const2 = lambda c: (0, 0)
    z_blk = 2 * d_mlstm // DS
    bc_blk = (2 * d_mlstm + 2 * DS) // GN2
    return pl.pallas_call(
        _ssd_kernel,
        out_shape=jax.ShapeDtypeStruct((bsz, nc * L, DS), BF16),
        grid=(nc,),
        in_specs=[
            pl.BlockSpec((bsz, L, DS), lambda c: (0, c, z_blk)),
            pl.BlockSpec((bsz, L, DS), lambda c: (0, c, z_blk + 1)),
            pl.BlockSpec((bsz, L, GN2), lambda c: (0, c, bc_blk)),
            pl.BlockSpec((bsz, L, LANES), lambda c: (0, c, 0)),
            pl.BlockSpec(conv_w.shape, const2),
            pl.BlockSpec(conv_b.shape, const2),
            pl.BlockSpec(dt_bias.shape, const2),
            pl.BlockSpec(a_log.shape, const2),
            pl.BlockSpec(dskip_x.shape, const2),
            pl.BlockSpec(norm_w.shape, const2),
        ],
        out_specs=pl.BlockSpec((bsz, L, DS), lambda c: (0, c, 0)),
        scratch_shapes=[
            pltpu.VMEM((bsz, HALO + L, DS + GN2), F32),
            pltpu.VMEM((bsz, HALO, DS + GN2), F32),
            pltpu.VMEM((bsz, SSD_STATE, DS), F32),
        ],
        compiler_params=_params(("arbitrary",)),
        name="ssd",
    )(proj, proj, proj, dt_raw, conv_w, conv_b, dt_bias, a_log, dskip_x, norm_w)


def _out_ln1_kernel(ym_ref, ys_ref, x_ref, g0_ref, b0_ref, wm_ref, ws_ref, g1_ref, b1_ref,
                    wr_ref, h1_ref, lg_ref):
    h0 = _layer_norm(x_ref[...], g0_ref[...], b0_ref[...])
    mix = (jnp.dot(ym_ref[...], wm_ref[...], preferred_element_type=F32)
           + jnp.dot(ys_ref[...], ws_ref[...], preferred_element_type=F32))
    h1 = _layer_norm(DEEPNORM_ALPHA * h0 + mix, g1_ref[...], b1_ref[...])
    h1_ref[...] = h1
    lg_ref[...] = _dot3(h1, wr_ref[...])


def _out_ln1(ym, ys, x2, g0, b0, w_o, g1, b1, w_r, tm):
    T, D = x2.shape
    DM = ym.shape[1]
    DS = ys.shape[1]
    assert DM % DS == 0
    row = lambda i: (i, 0)
    const = lambda i: (0, 0)
    return pl.pallas_call(
        _out_ln1_kernel,
        out_shape=(jax.ShapeDtypeStruct((T, D), F32), jax.ShapeDtypeStruct((T, LANES), F32)),
        grid=(T // tm,),
        in_specs=[
            pl.BlockSpec((tm, DM), row), pl.BlockSpec((tm, DS), row), pl.BlockSpec((tm, D), row),
            pl.BlockSpec((1, D), const), pl.BlockSpec((1, D), const),
            pl.BlockSpec((DM, D), const), pl.BlockSpec((DS, D), lambda i: (DM // DS, 0)),
            pl.BlockSpec((1, D), const), pl.BlockSpec((1, D), const),
            pl.BlockSpec((D, LANES), const),
        ],
        out_specs=(pl.BlockSpec((tm, D), row), pl.BlockSpec((tm, LANES), row)),
        compiler_params=_params(("parallel",)),
        name="out_ln1",
    )(ym, ys, x2, g0, b0, w_o, w_o, g1, b1, w_r)


def _moe_kernel(ie_ref, ir_ref, inb_ref, nit_ref, x_hbm, wg_ref, wu_ref, wd_ref, o_hbm,
                xbuf, acc, xsem, osem):
    i = pl.program_id(0)
    f = pl.program_id(1)
    nf = pl.num_programs(1)
    n_items = nit_ref[0]
    live = i < n_items
    slot = i % 2

    def x_copy(item, s, j):
        rows = pl.ds(pl.multiple_of(ir_ref[item] + j * MOE_BLOCK, MOE_BLOCK), MOE_BLOCK)
        return pltpu.make_async_copy(x_hbm.at[rows], xbuf.at[s, pl.ds(j * MOE_BLOCK, MOE_BLOCK)],
                                     xsem.at[s])

    def o_copy(item, s, j):
        rows = pl.ds(pl.multiple_of(ir_ref[item] + j * MOE_BLOCK, MOE_BLOCK), MOE_BLOCK)
        return pltpu.make_async_copy(acc.at[s, pl.ds(j * MOE_BLOCK, MOE_BLOCK)], o_hbm.at[rows],
                                     osem.at[s])

    def for_blocks(item, fn):
        n = inb_ref[item]
        for j in range(MOE_ITEM_BLOCKS):
            pl.when(j < n)(functools.partial(fn, j))

    @pl.when(jnp.logical_and(live, f == 0))
    def _():
        @pl.when(i == 0)
        def _():
            for_blocks(i, lambda j: x_copy(i, slot, j).start())

        @pl.when(i + 1 < n_items)
        def _():
            for_blocks(i + 1, lambda j: x_copy(i + 1, 1 - slot, j).start())

        for_blocks(i, lambda j: x_copy(i, slot, j).wait())

        @pl.when(i >= 2)
        def _():
            for_blocks(i - 2, lambda j: o_copy(i - 2, slot, j).wait())

    @pl.when(live)
    def _():
        def compute(n_rows):
            rows = pl.ds(0, n_rows)
            xb = xbuf[slot, rows, :].astype(BF16)
            a = jnp.dot(xb, wg_ref[0].astype(BF16), preferred_element_type=F32)
            u = jnp.dot(xb, wu_ref[0].astype(BF16), preferred_element_type=F32)
            part = jnp.dot((_silu(a) * u).astype(BF16), wd_ref[0].astype(BF16),
                           preferred_element_type=F32)

            @pl.when(f == 0)
            def _():
                acc[slot, rows, :] = part

            @pl.when(f > 0)
            def _():
                acc[slot, rows, :] += part

        for n in range(1, MOE_ITEM_BLOCKS + 1):
            pl.when(inb_ref[i] == n)(functools.partial(compute, n * MOE_BLOCK))

    @pl.when(jnp.logical_and(live, f == nf - 1))
    def _():
        for_blocks(i, lambda j: o_copy(i, slot, j).start())

        @pl.when(i == n_items - 1)
        def _():
            for_blocks(i, lambda j: o_copy(i, slot, j).wait())

            @pl.when(i >= 1)
            def _():
                for_blocks(i - 1, lambda j: o_copy(i - 1, 1 - slot, j).wait())

            zrows = pl.ds(0, MOE_BLOCK)
            xbuf[slot, zrows, :] = jnp.zeros((MOE_BLOCK, xbuf.shape[2]), xbuf.dtype)

            def z_copy(b):
                rows = pl.ds(pl.multiple_of(b * MOE_BLOCK, MOE_BLOCK), MOE_BLOCK)
                return pltpu.make_async_copy(xbuf.at[slot, zrows], o_hbm.at[rows], osem.at[slot])

            first, last = nit_ref[1], o_hbm.shape[0] // MOE_BLOCK
            lax.fori_loop(first, last, lambda b, c: (z_copy(b).start(), c)[1], 0)
            lax.fori_loop(first, last, lambda b, c: (z_copy(b).wait(), c)[1], 0)


def _moe_experts(item_e, item_row0, item_nblk, n_items, x_sorted, w_gate, w_up, w_down):
    NR, D = x_sorted.shape
    NI = item_e.shape[0]
    FF = w_gate.shape[2]
    tf = MOE_FF_TILE
    nf = FF // tf
    cap = MOE_ITEM_BLOCKS * MOE_BLOCK

    def expert(i, ie, nit):
        return ie[jnp.minimum(i, nit[0] - 1)]

    def ff(i, f, nit):
        return jnp.where(i < nit[0], f, nf - 1)

    return pl.pallas_call(
        _moe_kernel,
        out_shape=jax.ShapeDtypeStruct((NR, D), F32),
        grid_spec=pltpu.PrefetchScalarGridSpec(
            num_scalar_prefetch=4,
            grid=(NI, nf),
            in_specs=[
                pl.BlockSpec(memory_space=pl.ANY),
                pl.BlockSpec((1, D, tf),
                             lambda i, f, ie, ir, inb, nit: (expert(i, ie, nit), 0, ff(i, f, nit))),
                pl.BlockSpec((1, D, tf),
                             lambda i, f, ie, ir, inb, nit: (expert(i, ie, nit), 0, ff(i, f, nit))),
                pl.BlockSpec((1, tf, D),
                             lambda i, f, ie, ir, inb, nit: (expert(i, ie, nit), ff(i, f, nit), 0)),
            ],
            out_specs=pl.BlockSpec(memory_space=pl.ANY),
            scratch_shapes=[
                pltpu.VMEM((2, cap, D), F32),
                pltpu.VMEM((2, cap, D), F32),
                pltpu.SemaphoreType.DMA((2,)),
                pltpu.SemaphoreType.DMA((2,)),
            ],
        ),
        compiler_params=_params(("arbitrary", "arbitrary")),
        name="moe_experts",
    )(item_e, item_row0, item_nblk, n_items, x_sorted, w_gate, w_up, w_down)


def _combine_ln2_kernel(h_ref, pair_ref, w_ref, g_ref, b_ref, o_ref):
    D = h_ref.shape[1]
    w = w_ref[...]
    moe = pair_ref[:, 0:D] * w[:, 0:1]
    for k in range(1, TOP_K):
        moe = moe + pair_ref[:, k * D:(k + 1) * D] * w[:, k:k + 1]
    o_ref[...] = _layer_norm(DEEPNORM_ALPHA * h_ref[...] + moe, g_ref[...], b_ref[...])


def _combine_ln2(h1, pair, wts, g, b, tm):
    T, D = h1.shape
    row = lambda i: (i, 0)
    const = lambda i: (0, 0)
    return pl.pallas_call(
        _combine_ln2_kernel,
        out_shape=jax.ShapeDtypeStruct((T, D), F32),
        grid=(T // tm,),
        in_specs=[pl.BlockSpec((tm, D), row), pl.BlockSpec((tm, TOP_K * D), row),
                  pl.BlockSpec((tm, TOP_K), row),
                  pl.BlockSpec((1, D), const), pl.BlockSpec((1, D), const)],
        out_specs=pl.BlockSpec((tm, D), row),
        compiler_params=_params(("parallel",)),
        name="combine_ln2",
    )(h1, pair, wts, g, b)


def _block_diag_tiles(w):
    nb = w.shape[0]
    n_tiles = nb * QKV_BLOCK // MXU_DIM
    rows = w.reshape(n_tiles, MXU_DIM, QKV_BLOCK)
    tiled = jnp.tile(rows, (1, 1, MXU_DIM // QKV_BLOCK))
    r = jnp.arange(MXU_DIM)[:, None] // QKV_BLOCK
    c = jnp.arange(MXU_DIM)[None, :] // QKV_BLOCK
    return jnp.where(r == c, tiled, 0.0).astype(BF16)


def _pad_lanes(a, width=LANES):
    return jnp.pad(a, [(0, 0)] * (a.ndim - 1) + [(0, width - a.shape[-1])])


def _route(logits, b_rc, b_rf):
    T = logits.shape[0]
    pc = jax.nn.softmax(logits[:, :MOE_GROUPS] + b_rc, axis=-1)
    gidx = jnp.argmax(pc, axis=-1)
    pg = jnp.max(pc, axis=-1)
    lf = (logits[:, MOE_GROUPS:MOE_GROUPS + MOE_GROUPS * EXPERTS_PER_GROUP] + b_rf)
    lf = lf.reshape(T, MOE_GROUPS, EXPERTS_PER_GROUP)
    sel = jnp.take_along_axis(lf, gidx[:, None, None], axis=1)[:, 0]
    pf = jax.nn.softmax(sel, axis=-1)
    v2, i2 = lax.top_k(pf, TOP_K)
    wts = v2 / v2.sum(-1, keepdims=True) * pg[:, None]
    eid = gidx[:, None].astype(jnp.int32) * EXPERTS_PER_GROUP + i2.astype(jnp.int32)
    return eid, wts


def _dispatch(eid, n_experts):
    T = eid.shape[0]
    TK = T * TOP_K
    NR = (TK // MOE_BLOCK + n_experts + 1) * MOE_BLOCK
    e_flat = eid.reshape(TK)
    onehot = (e_flat[:, None] == jnp.arange(n_experts, dtype=jnp.int32)[None, :]).astype(jnp.int32)
    csum = jnp.cumsum(onehot, axis=0)
    counts = csum[-1]
    rank = jnp.sum(csum * onehot, axis=1) - 1
    padded = (counts + MOE_BLOCK - 1) // MOE_BLOCK * MOE_BLOCK
    pad_end = jnp.cumsum(padded)
    pad_start = pad_end - padded
    dest = pad_start[e_flat] + rank
    tok_flat = jnp.repeat(jnp.arange(T, dtype=jnp.int32), TOP_K)
    row_tok = (jnp.arange(NR, dtype=jnp.int32) % T).at[dest].set(tok_flat)
    cap = MOE_ITEM_BLOCKS * MOE_BLOCK
    n_it = (counts + cap - 1) // cap
    it_end = jnp.cumsum(n_it)
    it_start = it_end - n_it
    NI = TK // cap + n_experts
    idx = jnp.arange(NI, dtype=jnp.int32)
    item_e = jnp.minimum(jnp.searchsorted(it_end, idx, side='right'), n_experts - 1).astype(jnp.int32)
    k = idx - it_start[item_e]
    live = idx < it_end[-1]
    item_row0 = jnp.where(live, pad_start[item_e] + k * cap, 0).astype(jnp.int32)
    item_nblk = jnp.where(live, jnp.clip(padded[item_e] // MOE_BLOCK - k * MOE_ITEM_BLOCKS,
                                         0, MOE_ITEM_BLOCKS), 0).astype(jnp.int32)
    n_items = jnp.stack([it_end[-1], pad_end[-1] // MOE_BLOCK]).astype(jnp.int32)
    return dest.reshape(T, TOP_K), row_tok, item_e, item_row0, item_nblk, n_items


def kernel(x, ln_in_g, ln_in_b, w_in, conv_m_w, conv_m_b, w_q, w_k, w_v, w_if, b_if, mlstm_norm_w, mlstm_skip, conv_s_w, conv_s_b, dt_bias, a_log, d_skip, ssd_norm_w, w_out, ln1_g, ln1_b, w_router_coarse, b_router_coarse, w_router_fine, b_router_fine, w_gate_e, w_up_e, w_down_e, ln2_g, ln2_b):
    bsz, seq, D = x.shape
    T = bsz * seq
    nc = seq // CHUNK
    DM = conv_m_w.shape[-1]
    DS = ssd_norm_w.shape[-1]
    n_heads_s = DS // SSD_HEAD_DIM
    n_experts = w_gate_e.shape[1]
    assert w_in.shape[0] == DEPTH
    l = 0
    x2 = x.reshape(T, D)
    row = lambda a: a.reshape(1, -1)

    n_main = w_in.shape[-1] - n_heads_s
    w_dt = _pad_lanes(w_in[l, :, n_main:])
    proj, dt_raw = _ln_proj(x2, row(ln_in_g), row(ln_in_b), w_in[l].astype(BF16), n_main, w_dt,
                            tm=1024, tn=1024)
    proj = proj.reshape(bsz, seq, n_main)
    dt_raw = dt_raw.reshape(bsz, seq, LANES)

    wif = _pad_lanes(w_if[l]).astype(BF16).reshape(3, DM, LANES)
    y_m = _mlstm(proj, bsz, nc, conv_m_w[l], row(conv_m_b[l]),
                 _block_diag_tiles(w_q[l]), _block_diag_tiles(w_k[l]), _block_diag_tiles(w_v[l]),
                 wif, _pad_lanes(row(b_if[l])), row(mlstm_norm_w[l]), row(mlstm_skip[l]))
    y_m = y_m.reshape(T, DM)

    y_s = _ssd(proj, dt_raw, bsz, nc, DM, conv_s_w[l], row(conv_s_b[l]),
               _pad_lanes(row(dt_bias[l])), _pad_lanes(row(a_log[l])),
               row(jnp.repeat(d_skip[l], SSD_HEAD_DIM)), row(ssd_norm_w[l]))
    y_s = y_s.reshape(T, DS)

    w_o = w_out[l].astype(BF16)
    w_r = _pad_lanes(jnp.concatenate([w_router_coarse[l], w_router_fine[l]], axis=1))
    h1, logits = _out_ln1(y_m, y_s, x2, row(ln_in_g), row(ln_in_b), w_o,
                          row(ln1_g[l]), row(ln1_b[l]), w_r, tm=256)

    eid, wts = _route(logits, b_router_coarse[l], b_router_fine[l])
    dest, row_tok, item_e, item_row0, item_nblk, n_items = _dispatch(eid, n_experts)
    x_sorted = jnp.take(h1, row_tok, axis=0, mode='clip')
    yb = _moe_experts(item_e, item_row0, item_nblk, n_items, x_sorted,
                      w_gate_e[l], w_up_e[l], w_down_e[l])

    pair = jnp.take(yb, dest.reshape(T * TOP_K), axis=0, mode='clip').reshape(T, TOP_K * D)
    out = _combine_ln2(h1, pair, wts, row(ln2_g[l]), row(ln2_b[l]), tm=512)
    return out.reshape(bsz, seq, D)
```

```python
import functools
import math

import jax
import jax.numpy as jnp
from jax import lax
from jax.experimental import pallas as pl
from jax.experimental.pallas import tpu as pltpu

F32 = jnp.float32
BF16 = jnp.bfloat16

LANES = 128
MXU_DIM = 256
VMEM_LIMIT = 56 * 1024 * 1024

MLSTM_HEADS = 4
QKV_BLOCK = 4
CHUNK = 128
SSD_HEAD_DIM = 64
SSD_GROUPS = 4
SSD_STATE = 128
CONV_WIDTH = 4
MOE_GROUPS = 8
EXPERTS_PER_GROUP = 8
TOP_K = 2
MOE_BLOCK = 128
MOE_ITEM_BLOCKS = 4
MOE_FF_TILE = 512
DEPTH = 1
DEEPNORM_ALPHA = (2 * DEPTH) ** 0.25
LN_EPS = 1e-5
HALO = 8
LN_ROWS = 256


def _params(semantics):
    return pltpu.CompilerParams(dimension_semantics=semantics, vmem_limit_bytes=VMEM_LIMIT)


def _bdot(a, b):
    return jnp.dot(a.astype(BF16), b.astype(BF16), preferred_element_type=F32)


def _bdot_nt(a, b):
    return lax.dot_general(a.astype(BF16), b.astype(BF16), (((1,), (1,)), ((), ())),
                           preferred_element_type=F32)


def _split2(a):
    hi = a.astype(BF16)
    lo = (a - hi.astype(F32)).astype(BF16)
    return hi, lo


def _dot_exact_rhs(a, b01):
    hi, lo = _split2(a)
    return (jnp.dot(hi, b01, preferred_element_type=F32)
            + jnp.dot(lo, b01, preferred_element_type=F32))


def _dot_exact_lhs(a01, b):
    hi, lo = _split2(b)
    return (jnp.dot(a01, hi, preferred_element_type=F32)
            + jnp.dot(a01, lo, preferred_element_type=F32))


def _dot3(a, b):
    ah, al = _split2(a)
    bh, bl = _split2(b)
    return (jnp.dot(ah, bh, preferred_element_type=F32)
            + jnp.dot(al, bh, preferred_element_type=F32)
            + jnp.dot(ah, bl, preferred_element_type=F32))


def _sigmoid(x):
    return 1.0 / (1.0 + jnp.exp(-x))


def _silu(x):
    return x * _sigmoid(x)


def _softplus(x):
    return jnp.maximum(x, 0.0) + jnp.log1p(jnp.exp(-jnp.abs(x)))


def _layer_norm(x, g, b):
    mu = jnp.mean(x, axis=-1, keepdims=True)
    xc = x - mu
    var = jnp.mean(xc * xc, axis=-1, keepdims=True)
    return xc * lax.rsqrt(var + LN_EPS) * g + b


def _lower_tri(n, strict=False):
    r = lax.broadcasted_iota(jnp.int32, (n, n), 0)
    c = lax.broadcasted_iota(jnp.int32, (n, n), 1)
    return (r > c) if strict else (r >= c)


def _causal_conv(ext_ref, cur, halo_ref, w, b):
    L = cur.shape[0]
    ext_ref[0:HALO, :] = halo_ref[...]
    ext_ref[HALO:HALO + L, :] = cur
    halo_ref[...] = cur[L - HALO:, :]
    acc = b + w[CONV_WIDTH - 1:CONV_WIDTH, :] * cur
    for k in range(CONV_WIDTH - 1):
        off = HALO - (CONV_WIDTH - 1) + k
        acc = acc + w[k:k + 1, :] * ext_ref[off:off + L, :]
    return acc


def _ln_proj_kernel(x_ref, g_ref, b_ref, w_ref, wdt_ref, o_ref, dt_ref, hn_ref):
    j = pl.program_id(1)

    @pl.when(j == 0)
    def _():
        def chunk(r, carry):
            rows = pl.ds(pl.multiple_of(r * LN_ROWS, LN_ROWS), LN_ROWS)
            h = _layer_norm(x_ref[rows, :], g_ref[...], b_ref[...])
            hn_ref[rows, :] = h.astype(BF16)
            dt_ref[rows, :] = _dot3(h, wdt_ref[...])
            return carry

        lax.fori_loop(0, x_ref.shape[0] // LN_ROWS, chunk, 0)

    o_ref[...] = _bdot_nt(hn_ref[...], w_ref[...])


def _ln_proj(x2, g, b, w_in_t, n_main, w_dt, tm, tn):
    T, D = x2.shape
    N = n_main
    return pl.pallas_call(
        _ln_proj_kernel,
        out_shape=(jax.ShapeDtypeStruct((T, N), F32), jax.ShapeDtypeStruct((T, LANES), F32)),
        grid=(T // tm, N // tn),
        in_specs=[
            pl.BlockSpec((tm, D), lambda i, j: (i, 0)),
            pl.BlockSpec((1, D), lambda i, j: (0, 0)),
            pl.BlockSpec((1, D), lambda i, j: (0, 0)),
            pl.BlockSpec((tn, D), lambda i, j: (j, 0)),
            pl.BlockSpec((D, LANES), lambda i, j: (0, 0)),
        ],
        out_specs=(pl.BlockSpec((tm, tn), lambda i, j: (i, j)),
                   pl.BlockSpec((tm, LANES), lambda i, j: (i, 0))),
        scratch_shapes=[pltpu.VMEM((tm, D), BF16)],
        compiler_params=_params(("parallel", "arbitrary")),
        name="ln_proj",
    )(x2, g, b, w_in_t, w_dt)


def _mlstm_kernel(xm_ref, og_ref, cw_ref, cb_ref, wq_ref, wk_ref, wv_ref, wif_ref, bif_ref,
                  nw_ref, skip_ref, y_ref, ext_ref, halo_ref, ct_ref, n_ref, m_ref):
    @pl.when(pl.program_id(0) == 0)
    def _():
        halo_ref[...] = jnp.zeros_like(halo_ref)
        ct_ref[...] = jnp.zeros_like(ct_ref)
        n_ref[...] = jnp.zeros_like(n_ref)
        m_ref[...] = jnp.zeros_like(m_ref)

    for b in range(xm_ref.shape[0]):
        _mlstm_chunk(xm_ref.at[b], og_ref.at[b], cw_ref, cb_ref, wq_ref, wk_ref, wv_ref, wif_ref,
                     bif_ref, nw_ref, skip_ref, y_ref.at[b], ext_ref.at[b], halo_ref.at[b],
                     ct_ref.at[b], n_ref.at[b], m_ref.at[b])


def _mlstm_chunk(xm_ref, og_ref, cw_ref, cb_ref, wq_ref, wk_ref, wv_ref, wif_ref, bif_ref,
                 nw_ref, skip_ref, y_ref, ext_ref, halo_ref, ct_ref, n_ref, m_ref):
    L, DM = xm_ref.shape
    H = MLSTM_HEADS
    Dh = DM // H
    xm = xm_ref[...]
    xc = _silu(_causal_conv(ext_ref, xm, halo_ref, cw_ref[...], cb_ref[...]))
    xcb = xc.astype(BF16)
    xmb = xm.astype(BF16)
    nblk = DM // MXU_DIM
    q = jnp.concatenate([jnp.dot(xcb[:, i * MXU_DIM:(i + 1) * MXU_DIM], wq_ref[i],
                                 preferred_element_type=F32) for i in range(nblk)], axis=1)
    k = jnp.concatenate([jnp.dot(xcb[:, i * MXU_DIM:(i + 1) * MXU_DIM], wk_ref[i],
                                 preferred_element_type=F32) for i in range(nblk)], axis=1)
    v = jnp.concatenate([jnp.dot(xmb[:, i * MXU_DIM:(i + 1) * MXU_DIM], wv_ref[i],
                                 preferred_element_type=F32) for i in range(nblk)], axis=1)
    qb = q.astype(BF16)
    vb = v.astype(BF16)
    gates = (jnp.dot(qb, wif_ref[0], preferred_element_type=F32)
             + jnp.dot(k.astype(BF16), wif_ref[1], preferred_element_type=F32)
             + jnp.dot(vb, wif_ref[2], preferred_element_type=F32)) + bif_ref[...]
    logf = jnp.minimum(gates, 0.0) - jnp.log1p(jnp.exp(-jnp.abs(gates)))
    tri = _lower_tri(L).astype(BF16)
    bcum = _dot_exact_lhs(tri, logf)
    gates_t = gates.T
    bcum_t = bcum.T
    causal = _lower_tri(L)
    ks = k * (1.0 / math.sqrt(Dh))

    outs = []
    for h in range(H):
        sl = slice(h * Dh, (h + 1) * Dh)
        qh = qb[:, sl]
        kh = ks[:, sl]
        vh = vb[:, sl]
        bc = bcum[:, H + h:H + h + 1]
        br = bcum_t[H + h:H + h + 1, :]
        ic = gates[:, h:h + 1]
        ir = gates_t[h:h + 1, :]
        m_prev = m_ref[h:h + 1, 0:1]
        dlog = jnp.where(causal, bc - br + ir, -jnp.inf)
        inter = bc + m_prev
        mt = jnp.maximum(inter, jnp.max(dlog, axis=-1, keepdims=True))
        w_intra = jnp.exp(dlog - mt)
        w_inter = jnp.exp(inter - mt)
        sc = _bdot_nt(qh, kh) * w_intra
        ct = ct_ref[h]
        num = _bdot(sc, vh) + w_inter * _bdot(qh, ct)
        qn = jnp.sum(q[:, sl] * n_ref[h:h + 1, :], axis=-1, keepdims=True)
        den = jnp.sum(sc, axis=-1, keepdims=True) + w_inter * qn
        hb = num / jnp.maximum(jnp.abs(den), jnp.exp(-mt))
        mu = jnp.mean(hb, axis=-1, keepdims=True)
        hc = hb - mu
        var = jnp.mean(hc * hc, axis=-1, keepdims=True)
        outs.append(hc * lax.rsqrt(var + LN_EPS))
        bl = bc[L - 1:L, :]
        wlog = bl - bc + ic
        m_new = jnp.maximum(bl + m_prev, jnp.max(wlog, axis=0, keepdims=True))
        ws = jnp.exp(wlog - m_new)
        decay = jnp.exp(bl + m_prev - m_new)
        kw = kh * ws
        ct_ref[h] = decay * ct + lax.dot_general(
            kw.astype(BF16), vh, (((0,), (0,)), ((), ())), preferred_element_type=F32)
        n_ref[h:h + 1, :] = decay * n_ref[h:h + 1, :] + jnp.sum(kw, axis=0, keepdims=True)
        m_ref[h:h + 1, :] = jnp.broadcast_to(m_new, (1, m_ref.shape[1]))

    hn = jnp.concatenate(outs, axis=1) * nw_ref[...] + skip_ref[...] * xc
    y_ref[...] = (_sigmoid(og_ref[...]) * hn).astype(y_ref.dtype)


def _mlstm(proj, bsz, nc, conv_w, conv_b, wq_bd, wk_bd, wv_bd, wif, bif, norm_w, skip):
    L = CHUNK
    DM = conv_w.shape[1]
    Dh = DM // MLSTM_HEADS
    const2 = lambda c: (0, 0)
    const3 = lambda c: (0, 0, 0)
    return pl.pallas_call(
        _mlstm_kernel,
        out_shape=jax.ShapeDtypeStruct((bsz, nc * L, DM), BF16),
        grid=(nc,),
        in_specs=[
            pl.BlockSpec((bsz, L, DM), lambda c: (0, c, 0)),
            pl.BlockSpec((bsz, L, DM), lambda c: (0, c, 1)),
            pl.BlockSpec(conv_w.shape, const2),
            pl.BlockSpec(conv_b.shape, const2),
            pl.BlockSpec(wq_bd.shape, const3),
            pl.BlockSpec(wk_bd.shape, const3),
            pl.BlockSpec(wv_bd.shape, const3),
            pl.BlockSpec(wif.shape, const3),
            pl.BlockSpec(bif.shape, const2),
            pl.BlockSpec(norm_w.shape, const2),
            pl.BlockSpec(skip.shape, const2),
        ],
        out_specs=pl.BlockSpec((bsz, L, DM), lambda c: (0, c, 0)),
        scratch_shapes=[
            pltpu.VMEM((bsz, HALO + L, DM), F32),
            pltpu.VMEM((bsz, HALO, DM), F32),
            pltpu.VMEM((bsz, MLSTM_HEADS, Dh, Dh), F32),
            pltpu.VMEM((bsz, 8, Dh), F32),
            pltpu.VMEM((bsz, 8, LANES), F32),
        ],
        compiler_params=_params(("arbitrary",)),
        name="mlstm",
    )(proj, proj, conv_w, conv_b, wq_bd, wk_bd, wv_bd, wif, bif, norm_w, skip)


def _ssd_kernel(z_ref, xs_ref, bc_ref, dt_ref, cw_ref, cb_ref, dtb_ref, alog_ref, dskip_ref,
                nw_ref, y_ref, ext_ref, halo_ref, st_ref):
    @pl.when(pl.program_id(0) == 0)
    def _():
        halo_ref[...] = jnp.zeros_like(halo_ref)
        st_ref[...] = jnp.zeros_like(st_ref)

    for b in range(xs_ref.shape[0]):
        _ssd_chunk(z_ref.at[b], xs_ref.at[b], bc_ref.at[b], dt_ref.at[b], cw_ref, cb_ref, dtb_ref,
                   alog_ref, dskip_ref, nw_ref, y_ref.at[b], ext_ref.at[b], halo_ref.at[b],
                   st_ref.at[b])


def _ssd_chunk(z_ref, xs_ref, bc_ref, dt_ref, cw_ref, cb_ref, dtb_ref, alog_ref, dskip_ref,
               nw_ref, y_ref, ext_ref, halo_ref, st_ref):
    L, DS = xs_ref.shape
    G, N, P = SSD_GROUPS, SSD_STATE, SSD_HEAD_DIM
    GN = G * N
    DG = DS // G
    cur = jnp.concatenate([xs_ref[...], bc_ref[...]], axis=1)
    xbc = _silu(_causal_conv(ext_ref, cur, halo_ref, cw_ref[...], cb_ref[...]))
    xh = xbc[:, :DS]
    bm = xbc[:, DS:DS + GN]
    cm = xbc[:, DS + GN:]

    dt = _softplus(dt_ref[...] + dtb_ref[...])
    a = -jnp.exp(alog_ref[...])
    adt = dt * a
    tri = _lower_tri(L).astype(BF16)
    a_cs = _dot_exact_lhs(tri, adt)
    a_cs_t = a_cs.T
    ea = jnp.exp(a_cs)
    a_last = a_cs[L - 1:L, :]
    dstate = jnp.exp(a_last - a_cs)
    er = lax.broadcasted_iota(jnp.int32, (LANES, DS), 0)
    ec = lax.broadcasted_iota(jnp.int32, (LANES, DS), 1)
    expand = (ec // P == er).astype(BF16)
    stacked = jnp.concatenate([dt, ea, dstate], axis=0)
    ex = _dot_exact_rhs(stacked, expand)
    dt_x, ea_x, ds_x = ex[:L], ex[L:2 * L], ex[2 * L:]
    xd = xh * dt_x
    xdd = xd * ds_x
    xdb = xd.astype(BF16)
    causal = _lower_tri(L)
    lane = lax.broadcasted_iota(jnp.int32, (L, LANES), 1)
    lo_half = lane < P

    y_parts = []
    heads_per_group = DG // P
    for g in range(G):
        gs = slice(g * N, (g + 1) * N)
        cg = cm[:, gs].astype(BF16)
        bg = bm[:, gs].astype(BF16)
        cbg = _bdot_nt(cg, bg)
        cols = slice(g * DG, (g + 1) * DG)
        st = st_ref[:, cols]
        y_off = jnp.dot(cg, st.astype(BF16), preferred_element_type=F32) * ea_x[:, cols]
        yd = []
        for pr in range(heads_per_group // 2):
            h0 = g * heads_per_group + 2 * pr
            sc = []
            for h in (h0, h0 + 1):
                diff = a_cs[:, h:h + 1] - a_cs_t[h:h + 1, :]
                sc.append((cbg * jnp.exp(jnp.where(causal, diff, -jnp.inf))).astype(BF16))
            lhs = jnp.concatenate(sc, axis=1)
            xp = xdb[:, h0 * P:(h0 + 2) * P]
            zero = jnp.zeros_like(xp)
            rhs = jnp.concatenate([jnp.where(lo_half, xp, zero),
                                   jnp.where(lo_half, zero, xp)], axis=0)
            yd.append(jnp.dot(lhs, rhs, preferred_element_type=F32))
        y_parts.append(jnp.concatenate(yd, axis=1) + y_off)
        st_ref[:, cols] = ea_x[L - 1:L, cols] * st + lax.dot_general(
            bg, xdd[:, cols].astype(BF16), (((0,), (0,)), ((), ())), preferred_element_type=F32)

    y = jnp.concatenate(y_parts, axis=1) + dskip_ref[...] * xh
    y = y * _silu(z_ref[...])
    normed = []
    for g in range(G):
        yg = y[:, g * DG:(g + 1) * DG]
        normed.append(yg * lax.rsqrt(jnp.mean(yg * yg, axis=-1, keepdims=True) + LN_EPS))
    y_ref[...] = (jnp.concatenate(normed, axis=1) * nw_ref[...]).astype(y_ref.dtype)


def _ssd(proj, dt_raw, bsz, nc, d_mlstm, conv_w, conv_b, dt_bias, a_log, dskip_x, norm_w):
    L = CHUNK
    DS = norm_w.shape[1]
    GN2 = 2 * SSD_GROUPS * SSD_STATE
    const2 = lambda c: (0, 0)
    z_blk = 2 * d_mlstm // DS
    bc_blk = (2 * d_mlstm + 2 * DS) // GN2
    return pl.pallas_call(
        _ssd_kernel,
        out_shape=jax.ShapeDtypeStruct((bsz, nc * L, DS), BF16),
        grid=(nc,),
        in_specs=[
            pl.BlockSpec((bsz, L, DS), lambda c: (0, c, z_blk)),
            pl.BlockSpec((bsz, L, DS), lambda c: (0, c, z_blk + 1)),
            pl.BlockSpec((bsz, L, GN2), lambda c: (0, c, bc_blk)),
            pl.BlockSpec((bsz, L, LANES), lambda c: (0, c, 0)),
            pl.BlockSpec(conv_w.shape, const2),
            pl.BlockSpec(conv_b.shape, const2),
            pl.BlockSpec(dt_bias.shape, const2),
            pl.BlockSpec(a_log.shape, const2),
            pl.BlockSpec(dskip_x.shape, const2),
            pl.BlockSpec(norm_w.shape, const2),
        ],
        out_specs=pl.BlockSpec((bsz, L, DS), lambda c: (0, c, 0)),
        scratch_shapes=[
            pltpu.VMEM((bsz, HALO + L, DS + GN2), F32),
            pltpu.VMEM((bsz, HALO, DS + GN2), F32),
            pltpu.VMEM((bsz, SSD_STATE, DS), F32),
        ],
        compiler_params=_params(("arbitrary",)),
        name="ssd",
    )(proj, proj, proj, dt_raw, conv_w, conv_b, dt_bias, a_log, dskip_x, norm_w)


def _first_lane_where(cond, lane):
    return jnp.min(jnp.where(cond, lane, LANES), axis=-1, keepdims=True)


def _route_rows(lg):
    G, E = MOE_GROUPS, EXPERTS_PER_GROUP
    lane = lax.broadcasted_iota(jnp.int32, lg.shape, 1)
    neg = -jnp.inf
    coarse = lane < G
    lc = jnp.where(coarse, lg, neg)
    mc = jnp.max(lc, axis=-1, keepdims=True)
    pg = 1.0 / jnp.sum(jnp.exp(lc - mc), axis=-1, keepdims=True)
    gidx = _first_lane_where(jnp.logical_and(coarse, lc == mc), lane)
    lo = G + gidx * E
    sel = jnp.where(jnp.logical_and(lane >= lo, lane < lo + E), lg, neg)
    m1 = jnp.max(sel, axis=-1, keepdims=True)
    i1 = _first_lane_where(sel == m1, lane)
    sel2 = jnp.where(lane == i1, neg, sel)
    m2 = jnp.max(sel2, axis=-1, keepdims=True)
    i2 = _first_lane_where(sel2 == m2, lane)
    r = jnp.exp(m2 - m1)
    w1 = pg / (1.0 + r)
    w2 = pg * (r / (1.0 + r))
    out = jnp.where(lane == 0, (i1 - G).astype(F32), 0.0)
    out = jnp.where(lane == 1, (i2 - G).astype(F32), out)
    out = jnp.where(lane == 2, w1, out)
    return jnp.where(lane == 3, w2, out)


def _out_ln1_kernel(ym_ref, ys_ref, x_ref, g0_ref, b0_ref, wm_ref, ws_ref, g1_ref, b1_ref,
                    wr_ref, br_ref, h1_ref, rt_ref):
    h0 = _layer_norm(x_ref[...], g0_ref[...], b0_ref[...])
    mix = (jnp.dot(ym_ref[...], wm_ref[...], preferred_element_type=F32)
           + jnp.dot(ys_ref[...], ws_ref[...], preferred_element_type=F32))
    h1 = _layer_norm(DEEPNORM_ALPHA * h0 + mix, g1_ref[...], b1_ref[...])
    h1_ref[...] = h1
    rt_ref[...] = _route_rows(_dot3(h1, wr_ref[...]) + br_ref[...])


def _out_ln1(ym, ys, x2, g0, b0, w_o, g1, b1, w_r, b_r, tm):
    T, D = x2.shape
    DM = ym.shape[1]
    DS = ys.shape[1]
    assert DM % DS == 0
    row = lambda i: (i, 0)
    const = lambda i: (0, 0)
    return pl.pallas_call(
        _out_ln1_kernel,
        out_shape=(jax.ShapeDtypeStruct((T, D), F32), jax.ShapeDtypeStruct((T, LANES), F32)),
        grid=(T // tm,),
        in_specs=[
            pl.BlockSpec((tm, DM), row), pl.BlockSpec((tm, DS), row), pl.BlockSpec((tm, D), row),
            pl.BlockSpec((1, D), const), pl.BlockSpec((1, D), const),
            pl.BlockSpec((DM, D), const), pl.BlockSpec((DS, D), lambda i: (DM // DS, 0)),
            pl.BlockSpec((1, D), const), pl.BlockSpec((1, D), const),
            pl.BlockSpec((D, LANES), const), pl.BlockSpec((1, LANES), const),
        ],
        out_specs=(pl.BlockSpec((tm, D), row), pl.BlockSpec((tm, LANES), row)),
        compiler_params=_params(("parallel",)),
        name="out_ln1",
    )(ym, ys, x2, g0, b0, w_o, w_o, g1, b1, w_r, b_r)


def _moe_kernel(ie_ref, ir_ref, inb_ref, nit_ref, x_hbm, wg_ref, wu_ref, wd_ref, o_hbm,
                xbuf, acc, xsem, osem):
    i = pl.program_id(0)
    f = pl.program_id(1)
    nf = pl.num_programs(1)
    n_items = nit_ref[0]
    live = i < n_items
    slot = i % 2

    def x_copy(item, s, j):
        rows = pl.ds(pl.multiple_of(ir_ref[item] + j * MOE_BLOCK, MOE_BLOCK), MOE_BLOCK)
        return pltpu.make_async_copy(x_hbm.at[rows], xbuf.at[s, pl.ds(j * MOE_BLOCK, MOE_BLOCK)],
                                     xsem.at[s])

    def o_copy(item, s, j):
        rows = pl.ds(pl.multiple_of(ir_ref[item] + j * MOE_BLOCK, MOE_BLOCK), MOE_BLOCK)
        return pltpu.make_async_copy(acc.at[s, pl.ds(j * MOE_BLOCK, MOE_BLOCK)], o_hbm.at[rows],
                                     osem.at[s])

    def for_blocks(item, fn):
        n = inb_ref[item]
        for j in range(MOE_ITEM_BLOCKS):
            pl.when(j < n)(functools.partial(fn, j))

    @pl.when(jnp.logical_and(live, f == 0))
    def _():
        @pl.when(i == 0)
        def _():
            for_blocks(i, lambda j: x_copy(i, slot, j).start())

        @pl.when(i + 1 < n_items)
        def _():
            for_blocks(i + 1, lambda j: x_copy(i + 1, 1 - slot, j).start())

        for_blocks(i, lambda j: x_copy(i, slot, j).wait())

        @pl.when(i >= 2)
        def _():
            for_blocks(i - 2, lambda j: o_copy(i - 2, slot, j).wait())

    @pl.when(live)
    def _():
        def compute(n_rows):
            rows = pl.ds(0, n_rows)
            xb = xbuf[slot, rows, :].astype(BF16)
            a = jnp.dot(xb, wg_ref[0].astype(BF16), preferred_element_type=F32)
            u = jnp.dot(xb, wu_ref[0].astype(BF16), preferred_element_type=F32)
            part = jnp.dot((_silu(a) * u).astype(BF16), wd_ref[0].astype(BF16),
                           preferred_element_type=F32)

            @pl.when(f == 0)
            def _():
                acc[slot, rows, :] = part

            @pl.when(f > 0)
            def _():
                acc[slot, rows, :] += part

        for n in range(1, MOE_ITEM_BLOCKS + 1):
            pl.when(inb_ref[i] == n)(functools.partial(compute, n * MOE_BLOCK))

    @pl.when(jnp.logical_and(live, f == nf - 1))
    def _():
        for_blocks(i, lambda j: o_copy(i, slot, j).start())

        @pl.when(i == n_items - 1)
        def _():
            for_blocks(i, lambda j: o_copy(i, slot, j).wait())

            @pl.when(i >= 1)
            def _():
                for_blocks(i - 1, lambda j: o_copy(i - 1, 1 - slot, j).wait())

            zrows = pl.ds(0, MOE_BLOCK)
            xbuf[slot, zrows, :] = jnp.zeros((MOE_BLOCK, xbuf.shape[2]), xbuf.dtype)

            def z_copy(b):
                rows = pl.ds(pl.multiple_of(b * MOE_BLOCK, MOE_BLOCK), MOE_BLOCK)
                return pltpu.make_async_copy(xbuf.at[slot, zrows], o_hbm.at[rows], osem.at[slot])

            first, last = nit_ref[1], o_hbm.shape[0] // MOE_BLOCK
            lax.fori_loop(first, last, lambda b, c: (z_copy(b).start(), c)[1], 0)
            lax.fori_loop(first, last, lambda b, c: (z_copy(b).wait(), c)[1], 0)


def _moe_experts(item_e, item_row0, item_nblk, n_items, x_sorted, w_gate, w_up, w_down):
    NR, D = x_sorted.shape
    NI = item_e.shape[0]
    FF = w_gate.shape[2]
    tf = MOE_FF_TILE
    nf = FF // tf
    cap = MOE_ITEM_BLOCKS * MOE_BLOCK

    def expert(i, ie, nit):
        return ie[jnp.minimum(i, nit[0] - 1)]

    def ff(i, f, nit):
        return jnp.where(i < nit[0], f, nf - 1)

    return pl.pallas_call(
        _moe_kernel,
        out_shape=jax.ShapeDtypeStruct((NR, D), F32),
        grid_spec=pltpu.PrefetchScalarGridSpec(
            num_scalar_prefetch=4,
            grid=(NI, nf),
            in_specs=[
                pl.BlockSpec(memory_space=pl.ANY),
                pl.BlockSpec((1, D, tf),
                             lambda i, f, ie, ir, inb, nit: (expert(i, ie, nit), 0, ff(i, f, nit))),
                pl.BlockSpec((1, D, tf),
                             lambda i, f, ie, ir, inb, nit: (expert(i, ie, nit), 0, ff(i, f, nit))),
                pl.BlockSpec((1, tf, D),
                             lambda i, f, ie, ir, inb, nit: (expert(i, ie, nit), ff(i, f, nit), 0)),
            ],
            out_specs=pl.BlockSpec(memory_space=pl.ANY),
            scratch_shapes=[
                pltpu.VMEM((2, cap, D), F32),
                pltpu.VMEM((2, cap, D), F32),
                pltpu.SemaphoreType.DMA((2,)),
                pltpu.SemaphoreType.DMA((2,)),
            ],
        ),
        compiler_params=_params(("arbitrary", "arbitrary")),
        name="moe_experts",
    )(item_e, item_row0, item_nblk, n_items, x_sorted, w_gate, w_up, w_down)


def _combine_ln2_kernel(h_ref, *refs):
    m_refs, (w_ref, g_ref, b_ref, o_ref) = refs[:TOP_K], refs[TOP_K:]
    w = w_ref[...]
    moe = m_refs[0][...] * w[:, 0:1]
    for k in range(1, TOP_K):
        moe = moe + m_refs[k][...] * w[:, k:k + 1]
    o_ref[...] = _layer_norm(DEEPNORM_ALPHA * h_ref[...] + moe, g_ref[...], b_ref[...])


def _combine_ln2(h1, picked, wts, g, b, tm):
    T, D = h1.shape
    row = lambda i: (i, 0)
    const = lambda i: (0, 0)
    kth = [pl.BlockSpec((tm, D), functools.partial(lambda k, i: (k * (T // tm) + i, 0), k))
           for k in range(TOP_K)]
    return pl.pallas_call(
        _combine_ln2_kernel,
        out_shape=jax.ShapeDtypeStruct((T, D), F32),
        grid=(T // tm,),
        in_specs=[pl.BlockSpec((tm, D), row), *kth, pl.BlockSpec((tm, TOP_K), row),
                  pl.BlockSpec((1, D), const), pl.BlockSpec((1, D), const)],
        out_specs=pl.BlockSpec((tm, D), row),
        compiler_params=_params(("parallel",)),
        name="combine_ln2",
    )(h1, *([picked] * TOP_K), wts, g, b)


def _block_diag_tiles(w):
    nb = w.shape[0]
    n_tiles = nb * QKV_BLOCK // MXU_DIM
    rows = w.reshape(n_tiles, MXU_DIM, QKV_BLOCK)
    tiled = jnp.tile(rows, (1, 1, MXU_DIM // QKV_BLOCK))
    r = jnp.arange(MXU_DIM)[:, None] // QKV_BLOCK
    c = jnp.arange(MXU_DIM)[None, :] // QKV_BLOCK
    return jnp.where(r == c, tiled, 0.0).astype(BF16)


def _pad_lanes(a, width=LANES):
    return jnp.pad(a, [(0, 0)] * (a.ndim - 1) + [(0, width - a.shape[-1])])


def _dispatch_kernel(e_ref, dest_ref, cnt_ref, rank_ref):
    R = e_ref.shape[0]
    sub = lax.broadcasted_iota(jnp.int32, (LANES, LANES), 0)
    lanes = lax.broadcasted_iota(jnp.int32, (LANES, LANES), 1)
    incl = (sub <= lanes).astype(BF16)

    def onehot_t(b):
        return sub == e_ref[b]

    def rank_pass(b, base):
        ot = onehot_t(b)
        otf = ot.astype(F32)
        seen = jnp.dot(ot.astype(BF16), incl, preferred_element_type=F32) + base
        rank_ref[b] = jnp.sum(jnp.where(ot, seen, 0.0), axis=0, keepdims=True) - 1.0
        return base + jnp.sum(otf, axis=1, keepdims=True)

    counts = lax.fori_loop(0, R, rank_pass, jnp.zeros((LANES, 1), F32))
    padded = jnp.ceil(counts * (1.0 / MOE_BLOCK)) * MOE_BLOCK
    strict = (lanes < sub).astype(BF16)
    start = _dot_exact_lhs(strict, jnp.broadcast_to(padded, (LANES, LANES)))[:, 0:1]

    def dest_pass(b, carry):
        dest_ref[b] = (jnp.sum(jnp.where(onehot_t(b), start, 0.0), axis=0, keepdims=True)
                       + rank_ref[b]).astype(jnp.int32)
        return carry

    lax.fori_loop(0, R, dest_pass, 0)
    cnt_ref[...] = jnp.broadcast_to(counts, (LANES, LANES)).T[0:8, :].astype(jnp.int32)


def _dispatch_dest(e_flat):
    TK = e_flat.shape[0]
    R = TK // LANES
    dest, counts = pl.pallas_call(
        _dispatch_kernel,
        out_shape=(jax.ShapeDtypeStruct((R, 1, LANES), jnp.int32),
                   jax.ShapeDtypeStruct((8, LANES), jnp.int32)),
        scratch_shapes=[pltpu.VMEM((R, 1, LANES), F32)],
        name="dispatch",
    )(e_flat.reshape(R, 1, LANES))
    return dest.reshape(TK), counts[0]


def _dispatch(e_flat, T, n_experts):
    TK = T * TOP_K
    NR = (TK // MOE_BLOCK + n_experts + 1) * MOE_BLOCK
    dest, counts = _dispatch_dest(e_flat)
    counts = counts[:n_experts]
    padded = (counts + MOE_BLOCK - 1) // MOE_BLOCK * MOE_BLOCK
    pad_end = jnp.cumsum(padded)
    pad_start = pad_end - padded
    tok_flat = jnp.tile(jnp.arange(T, dtype=jnp.int32), TOP_K)
    row_tok = (jnp.arange(NR, dtype=jnp.int32) % T).at[dest].set(tok_flat)
    cap = MOE_ITEM_BLOCKS * MOE_BLOCK
    n_it = (counts + cap - 1) // cap
    it_end = jnp.cumsum(n_it)
    it_start = it_end - n_it
    NI = TK // cap + n_experts
    idx = jnp.arange(NI, dtype=jnp.int32)
    item_e = jnp.minimum(jnp.searchsorted(it_end, idx, side='right'), n_experts - 1).astype(jnp.int32)
    k = idx - it_start[item_e]
    live = idx < it_end[-1]
    item_row0 = jnp.where(live, pad_start[item_e] + k * cap, 0).astype(jnp.int32)
    item_nblk = jnp.where(live, jnp.clip(padded[item_e] // MOE_BLOCK - k * MOE_ITEM_BLOCKS,
                                         0, MOE_ITEM_BLOCKS), 0).astype(jnp.int32)
    n_items = jnp.stack([it_end[-1], pad_end[-1] // MOE_BLOCK]).astype(jnp.int32)
    return dest, row_tok, item_e, item_row0, item_nblk, n_items


def kernel(x, ln_in_g, ln_in_b, w_in, conv_m_w, conv_m_b, w_q, w_k, w_v, w_if, b_if, mlstm_norm_w, mlstm_skip, conv_s_w, conv_s_b, dt_bias, a_log, d_skip, ssd_norm_w, w_out, ln1_g, ln1_b, w_router_coarse, b_router_coarse, w_router_fine, b_router_fine, w_gate_e, w_up_e, w_down_e, ln2_g, ln2_b):
    bsz, seq, D = x.shape
    T = bsz * seq
    nc = seq // CHUNK
    DM = conv_m_w.shape[-1]
    DS = ssd_norm_w.shape[-1]
    n_heads_s = DS // SSD_HEAD_DIM
    n_experts = w_gate_e.shape[1]
    assert w_in.shape[0] == DEPTH
    l = 0
    x2 = x.reshape(T, D)
    row = lambda a: a.reshape(1, -1)

    n_main = w_in.shape[-1] - n_heads_s
    w_dt = _pad_lanes(w_in[l, :, n_main:])
    w_in_t = jnp.swapaxes(w_in[l], 0, 1).astype(BF16)
    proj, dt_raw = _ln_proj(x2, row(ln_in_g), row(ln_in_b), w_in_t, n_main, w_dt, tm=1024, tn=1024)
    proj = proj.reshape(bsz, seq, n_main)
    dt_raw = dt_raw.reshape(bsz, seq, LANES)

    wif = _pad_lanes(w_if[l]).astype(BF16).reshape(3, DM, LANES)
    y_m = _mlstm(proj, bsz, nc, conv_m_w[l], row(conv_m_b[l]),
                 _block_diag_tiles(w_q[l]), _block_diag_tiles(w_k[l]), _block_diag_tiles(w_v[l]),
                 wif, _pad_lanes(row(b_if[l])), row(mlstm_norm_w[l]), row(mlstm_skip[l]))
    y_m = y_m.reshape(T, DM)

    y_s = _ssd(proj, dt_raw, bsz, nc, DM, conv_s_w[l], row(conv_s_b[l]),
               _pad_lanes(row(dt_bias[l])), _pad_lanes(row(a_log[l])),
               row(jnp.repeat(d_skip[l], SSD_HEAD_DIM)), row(ssd_norm_w[l]))
    y_s = y_s.reshape(T, DS)

    w_o = w_out[l].astype(BF16)
    w_r = _pad_lanes(jnp.concatenate([w_router_coarse[l], w_router_fine[l]], axis=1))
    b_r = _pad_lanes(row(jnp.concatenate([b_router_coarse[l], b_router_fine[l]])))
    h1, routed = _out_ln1(y_m, y_s, x2, row(ln_in_g), row(ln_in_b), w_o,
                          row(ln1_g[l]), row(ln1_b[l]), w_r, b_r, tm=256)

    e_flat = routed[:, :TOP_K].T.reshape(TOP_K * T).astype(jnp.int32)
    wts = routed[:, TOP_K:2 * TOP_K]
    dest, row_tok, item_e, item_row0, item_nblk, n_items = _dispatch(e_flat, T, n_experts)
    x_sorted = jnp.take(h1, row_tok, axis=0, mode='clip')
    yb = _moe_experts(item_e, item_row0, item_nblk, n_items, x_sorted,
                      w_gate_e[l], w_up_e[l], w_down_e[l])

    picked = jnp.take(yb, dest, axis=0, mode='clip')
    out = _combine_ln2(h1, picked, wts, row(ln2_g[l]), row(ln2_b[l]), tm=512)
    return out.reshape(bsz, seq, D)
```

```python
import functools
import math

import jax
import jax.numpy as jnp
from jax import lax
from jax.experimental import pallas as pl
from jax.experimental.pallas import tpu as pltpu

F32 = jnp.float32
BF16 = jnp.bfloat16

LANES = 128
MXU_DIM = 256
VMEM_LIMIT = 56 * 1024 * 1024

MLSTM_HEADS = 4
QKV_BLOCK = 4
CHUNK = 128
SSD_HEAD_DIM = 64
SSD_GROUPS = 4
SSD_STATE = 128
CONV_WIDTH = 4
MOE_GROUPS = 8
EXPERTS_PER_GROUP = 8
TOP_K = 2
MOE_BLOCK = 128
MOE_ITEM_BLOCKS = 4
MOE_FF_TILE = 512
DEPTH = 1
DEEPNORM_ALPHA = (2 * DEPTH) ** 0.25
LN_EPS = 1e-5
HALO = 8
LN_ROWS = 256
OUT_ROWS = 256


def _params(semantics):
    return pltpu.CompilerParams(dimension_semantics=semantics, vmem_limit_bytes=VMEM_LIMIT)


def _bdot(a, b):
    return jnp.dot(a.astype(BF16), b.astype(BF16), preferred_element_type=F32)


def _bdot_nt(a, b):
    return lax.dot_general(a.astype(BF16), b.astype(BF16), (((1,), (1,)), ((), ())),
                           preferred_element_type=F32)


def _split2(a):
    hi = a.astype(BF16)
    lo = (a - hi.astype(F32)).astype(BF16)
    return hi, lo


def _dot_exact_rhs(a, b01):
    hi, lo = _split2(a)
    return (jnp.dot(hi, b01, preferred_element_type=F32)
            + jnp.dot(lo, b01, preferred_element_type=F32))


def _dot_exact_lhs(a01, b):
    hi, lo = _split2(b)
    return (jnp.dot(a01, hi, preferred_element_type=F32)
            + jnp.dot(a01, lo, preferred_element_type=F32))


def _dot3(a, b):
    ah, al = _split2(a)
    bh, bl = _split2(b)
    return (jnp.dot(ah, bh, preferred_element_type=F32)
            + jnp.dot(al, bh, preferred_element_type=F32)
            + jnp.dot(ah, bl, preferred_element_type=F32))


def _sigmoid(x):
    return 1.0 / (1.0 + jnp.exp(-x))


def _silu(x):
    return x * _sigmoid(x)


def _softplus(x):
    return jnp.maximum(x, 0.0) + jnp.log1p(jnp.exp(-jnp.abs(x)))


def _layer_norm(x, g, b):
    mu = jnp.mean(x, axis=-1, keepdims=True)
    xc = x - mu
    var = jnp.mean(xc * xc, axis=-1, keepdims=True)
    return xc * lax.rsqrt(var + LN_EPS) * g + b


def _lower_tri(n, strict=False):
    r = lax.broadcasted_iota(jnp.int32, (n, n), 0)
    c = lax.broadcasted_iota(jnp.int32, (n, n), 1)
    return (r > c) if strict else (r >= c)


def _causal_conv(cur, halo_ref, w, b):
    L = cur.shape[0]
    ext = jnp.concatenate([halo_ref[...], cur], axis=0)
    halo_ref[...] = cur[L - HALO:, :]
    acc = b + w[CONV_WIDTH - 1:CONV_WIDTH, :] * cur
    for s in range(1, CONV_WIDTH):
        delayed = pltpu.roll(ext, s, axis=0)[HALO:, :]
        acc = acc + w[CONV_WIDTH - 1 - s:CONV_WIDTH - s, :] * delayed
    return acc


def _ln_proj_kernel(x_ref, g_ref, b_ref, w_ref, wdt_ref, o_ref, dt_ref, hn_ref):
    j = pl.program_id(1)

    @pl.when(j == 0)
    def _():
        def chunk(r, carry):
            rows = pl.ds(pl.multiple_of(r * LN_ROWS, LN_ROWS), LN_ROWS)
            h = _layer_norm(x_ref[rows, :], g_ref[...], b_ref[...])
            hn_ref[rows, :] = h.astype(BF16)
            dt_ref[rows, :] = _dot3(h, wdt_ref[...])
            return carry

        lax.fori_loop(0, x_ref.shape[0] // LN_ROWS, chunk, 0)

    o_ref[...] = _bdot_nt(hn_ref[...], w_ref[...])


def _ln_proj(x2, g, b, w_in_t, n_main, w_dt, tm, tn):
    T, D = x2.shape
    N = n_main
    return pl.pallas_call(
        _ln_proj_kernel,
        out_shape=(jax.ShapeDtypeStruct((T, N), F32), jax.ShapeDtypeStruct((T, LANES), F32)),
        grid=(T // tm, N // tn),
        in_specs=[
            pl.BlockSpec((tm, D), lambda i, j: (i, 0)),
            pl.BlockSpec((1, D), lambda i, j: (0, 0)),
            pl.BlockSpec((1, D), lambda i, j: (0, 0)),
            pl.BlockSpec((tn, D), lambda i, j: (j, 0)),
            pl.BlockSpec((D, LANES), lambda i, j: (0, 0)),
        ],
        out_specs=(pl.BlockSpec((tm, tn), lambda i, j: (i, j)),
                   pl.BlockSpec((tm, LANES), lambda i, j: (i, 0))),
        scratch_shapes=[pltpu.VMEM((tm, D), BF16)],
        compiler_params=_params(("parallel", "arbitrary")),
        name="ln_proj",
    )(x2, g, b, w_in_t, w_dt)


def _mlstm_kernel(xm_ref, og_ref, cw_ref, cb_ref, wq_ref, wk_ref, wv_ref, wif_ref, bif_ref,
                  nw_ref, skip_ref, y_ref, halo_ref, ct_ref, n_ref, m_ref):
    @pl.when(pl.program_id(0) == 0)
    def _():
        halo_ref[...] = jnp.zeros_like(halo_ref)
        ct_ref[...] = jnp.zeros_like(ct_ref)
        n_ref[...] = jnp.zeros_like(n_ref)
        m_ref[...] = jnp.zeros_like(m_ref)

    for b in range(xm_ref.shape[0]):
        _mlstm_chunk(xm_ref.at[b], og_ref.at[b], cw_ref, cb_ref, wq_ref, wk_ref, wv_ref, wif_ref,
                     bif_ref, nw_ref, skip_ref, y_ref.at[b], halo_ref.at[b],
                     ct_ref.at[b], n_ref.at[b], m_ref.at[b])


def _mlstm_chunk(xm_ref, og_ref, cw_ref, cb_ref, wq_ref, wk_ref, wv_ref, wif_ref, bif_ref,
                 nw_ref, skip_ref, y_ref, halo_ref, ct_ref, n_ref, m_ref):
    L, DM = xm_ref.shape
    H = MLSTM_HEADS
    Dh = DM // H
    xm = xm_ref[...]
    xc = _silu(_causal_conv(xm, halo_ref, cw_ref[...], cb_ref[...]))
    xcb = xc.astype(BF16)
    xmb = xm.astype(BF16)
    nblk = DM // MXU_DIM
    q = jnp.concatenate([jnp.dot(xcb[:, i * MXU_DIM:(i + 1) * MXU_DIM], wq_ref[i],
                                 preferred_element_type=F32) for i in range(nblk)], axis=1)
    k = jnp.concatenate([jnp.dot(xcb[:, i * MXU_DIM:(i + 1) * MXU_DIM], wk_ref[i],
                                 preferred_element_type=F32) for i in range(nblk)], axis=1)
    v = jnp.concatenate([jnp.dot(xmb[:, i * MXU_DIM:(i + 1) * MXU_DIM], wv_ref[i],
                                 preferred_element_type=F32) for i in range(nblk)], axis=1)
    qb = q.astype(BF16)
    vb = v.astype(BF16)
    gates = (jnp.dot(qb, wif_ref[0], preferred_element_type=F32)
             + jnp.dot(k.astype(BF16), wif_ref[1], preferred_element_type=F32)
             + jnp.dot(vb, wif_ref[2], preferred_element_type=F32)) + bif_ref[...]
    logf = jnp.minimum(gates, 0.0) - jnp.log1p(jnp.exp(-jnp.abs(gates)))
    tri = _lower_tri(L).astype(BF16)
    bcum = _dot_exact_lhs(tri, logf)
    gates_t = gates.T
    bcum_t = bcum.T
    causal = _lower_tri(L)
    ks = k * (1.0 / math.sqrt(Dh))

    outs = []
    for h in range(H):
        sl = slice(h * Dh, (h + 1) * Dh)
        qh = qb[:, sl]
        kh = ks[:, sl]
        vh = vb[:, sl]
        bc = bcum[:, H + h:H + h + 1]
        br = bcum_t[H + h:H + h + 1, :]
        ic = gates[:, h:h + 1]
        ir = gates_t[h:h + 1, :]
        m_prev = m_ref[h:h + 1, 0:1]
        dlog = jnp.where(causal, bc - br + ir, -jnp.inf)
        inter = bc + m_prev
        mt = jnp.maximum(inter, jnp.max(dlog, axis=-1, keepdims=True))
        w_intra = jnp.exp(dlog - mt)
        w_inter = jnp.exp(inter - mt)
        sc = _bdot_nt(qh, kh) * w_intra
        ct = ct_ref[h]
        num = _bdot(sc, vh) + w_inter * _bdot(qh, ct)
        qn = jnp.sum(q[:, sl] * n_ref[h:h + 1, :], axis=-1, keepdims=True)
        den = jnp.sum(sc, axis=-1, keepdims=True) + w_inter * qn
        hb = num / jnp.maximum(jnp.abs(den), jnp.exp(-mt))
        mu = jnp.mean(hb, axis=-1, keepdims=True)
        hc = hb - mu
        var = jnp.mean(hc * hc, axis=-1, keepdims=True)
        outs.append(hc * lax.rsqrt(var + LN_EPS))
        bl = bc[L - 1:L, :]
        wlog = bl - bc + ic
        m_new = jnp.maximum(bl + m_prev, jnp.max(wlog, axis=0, keepdims=True))
        ws = jnp.exp(wlog - m_new)
        decay = jnp.exp(bl + m_prev - m_new)
        kw = kh * ws
        ct_ref[h] = decay * ct + lax.dot_general(
            kw.astype(BF16), vh, (((0,), (0,)), ((), ())), preferred_element_type=F32)
        n_ref[h:h + 1, :] = decay * n_ref[h:h + 1, :] + jnp.sum(kw, axis=0, keepdims=True)
        m_ref[h:h + 1, :] = jnp.broadcast_to(m_new, (1, m_ref.shape[1]))

    hn = jnp.concatenate(outs, axis=1) * nw_ref[...] + skip_ref[...] * xc
    y_ref[...] = (_sigmoid(og_ref[...]) * hn).astype(y_ref.dtype)


def _mlstm(proj, bsz, nc, conv_w, conv_b, wq_bd, wk_bd, wv_bd, wif, bif, norm_w, skip):
    L = CHUNK
    DM = conv_w.shape[1]
    Dh = DM // MLSTM_HEADS
    const2 = lambda c: (0, 0)
    const3 = lambda c: (0, 0, 0)
    return pl.pallas_call(
        _mlstm_kernel,
        out_shape=jax.ShapeDtypeStruct((bsz, nc * L, DM), BF16),
        grid=(nc,),
        in_specs=[
            pl.BlockSpec((bsz, L, DM), lambda c: (0, c, 0)),
            pl.BlockSpec((bsz, L, DM), lambda c: (0, c, 1)),
            pl.BlockSpec(conv_w.shape, const2),
            pl.BlockSpec(conv_b.shape, const2),
            pl.BlockSpec(wq_bd.shape, const3),
            pl.BlockSpec(wk_bd.shape, const3),
            pl.BlockSpec(wv_bd.shape, const3),
            pl.BlockSpec(wif.shape, const3),
            pl.BlockSpec(bif.shape, const2),
            pl.BlockSpec(norm_w.shape, const2),
            pl.BlockSpec(skip.shape, const2),
        ],
        out_specs=pl.BlockSpec((bsz, L, DM), lambda c: (0, c, 0)),
        scratch_shapes=[
            pltpu.VMEM((bsz, HALO, DM), F32),
            pltpu.VMEM((bsz, MLSTM_HEADS, Dh, Dh), F32),
            pltpu.VMEM((bsz, 8, Dh), F32),
            pltpu.VMEM((bsz, 8, LANES), F32),
        ],
        compiler_params=_params(("arbitrary",)),
        name="mlstm",
    )(proj, proj, conv_w, conv_b, wq_bd, wk_bd, wv_bd, wif, bif, norm_w, skip)


def _ssd_kernel(z_ref, xs_ref, bc_ref, dt_ref, cw_ref, cb_ref, dtb_ref, alog_ref, dskip_ref,
                nw_ref, y_ref, halo_ref, st_ref):
    @pl.when(pl.program_id(0) == 0)
    def _():
        halo_ref[...] = jnp.zeros_like(halo_ref)
        st_ref[...] = jnp.zeros_like(st_ref)

    for b in range(xs_ref.shape[0]):
        _ssd_chunk(z_ref.at[b], xs_ref.at[b], bc_ref.at[b], dt_ref.at[b], cw_ref, cb_ref, dtb_ref,
                   alog_ref, dskip_ref, nw_ref, y_ref.at[b], halo_ref.at[b],
                   st_ref.at[b])


def _ssd_chunk(z_ref, xs_ref, bc_ref, dt_ref, cw_ref, cb_ref, dtb_ref, alog_ref, dskip_ref,
               nw_ref, y_ref, halo_ref, st_ref):
    L, DS = xs_ref.shape
    G, N, P = SSD_GROUPS, SSD_STATE, SSD_HEAD_DIM
    GN = G * N
    DG = DS // G
    cur = jnp.concatenate([xs_ref[...], bc_ref[...]], axis=1)
    xbc = _silu(_causal_conv(cur, halo_ref, cw_ref[...], cb_ref[...]))
    xh = xbc[:, :DS]
    bm = xbc[:, DS:DS + GN]
    cm = xbc[:, DS + GN:]

    dt = _softplus(dt_ref[...] + dtb_ref[...])
    a = -jnp.exp(alog_ref[...])
    adt = dt * a
    tri = _lower_tri(L).astype(BF16)
    a_cs = _dot_exact_lhs(tri, adt)
    a_cs_t = a_cs.T
    ea = jnp.exp(a_cs)
    a_last = a_cs[L - 1:L, :]
    dstate = jnp.exp(a_last - a_cs)
    er = lax.broadcasted_iota(jnp.int32, (LANES, DS), 0)
    ec = lax.broadcasted_iota(jnp.int32, (LANES, DS), 1)
    expand = (ec // P == er).astype(BF16)
    stacked = jnp.concatenate([dt, ea, dstate], axis=0)
    ex = _dot_exact_rhs(stacked, expand)
    dt_x, ea_x, ds_x = ex[:L], ex[L:2 * L], ex[2 * L:]
    xd = xh * dt_x
    xdd = xd * ds_x
    xdb = xd.astype(BF16)
    causal = _lower_tri(L)
    lane = lax.broadcasted_iota(jnp.int32, (L, LANES), 1)
    lo_half = lane < P

    y_parts = []
    heads_per_group = DG // P
    for g in range(G):
        gs = slice(g * N, (g + 1) * N)
        cg = cm[:, gs].astype(BF16)
        bg = bm[:, gs].astype(BF16)
        cbg = _bdot_nt(cg, bg)
        cols = slice(g * DG, (g + 1) * DG)
        st = st_ref[:, cols]
        y_off = jnp.dot(cg, st.astype(BF16), preferred_element_type=F32) * ea_x[:, cols]
        yd = []
        for pr in range(heads_per_group // 2):
            h0 = g * heads_per_group + 2 * pr
            sc = []
            for h in (h0, h0 + 1):
                diff = a_cs[:, h:h + 1] - a_cs_t[h:h + 1, :]
                sc.append((cbg * jnp.exp(jnp.where(causal, diff, -jnp.inf))).astype(BF16))
            lhs = jnp.concatenate(sc, axis=1)
            xp = xdb[:, h0 * P:(h0 + 2) * P]
            zero = jnp.zeros_like(xp)
            rhs = jnp.concatenate([jnp.where(lo_half, xp, zero),
                                   jnp.where(lo_half, zero, xp)], axis=0)
            yd.append(jnp.dot(lhs, rhs, preferred_element_type=F32))
        y_parts.append(jnp.concatenate(yd, axis=1) + y_off)
        st_ref[:, cols] = ea_x[L - 1:L, cols] * st + lax.dot_general(
            bg, xdd[:, cols].astype(BF16), (((0,), (0,)), ((), ())), preferred_element_type=F32)

    y = jnp.concatenate(y_parts, axis=1) + dskip_ref[...] * xh
    y = y * _silu(z_ref[...])
    normed = []
    for g in range(G):
        yg = y[:, g * DG:(g + 1) * DG]
        normed.append(yg * lax.rsqrt(jnp.mean(yg * yg, axis=-1, keepdims=True) + LN_EPS))
    y_ref[...] = (jnp.concatenate(normed, axis=1) * nw_ref[...]).astype(y_ref.dtype)


def _ssd(proj, dt_raw, bsz, nc, d_mlstm, conv_w, conv_b, dt_bias, a_log, dskip_x, norm_w):
    L = CHUNK
    DS = norm_w.shape[1]
    GN2 = 2 * SSD_GROUPS * SSD_STATE
    const2 = lambda c: (0, 0)
    z_blk = 2 * d_mlstm // DS
    bc_blk = (2 * d_mlstm + 2 * DS) // GN2
    return pl.pallas_call(
        _ssd_kernel,
        out_shape=jax.ShapeDtypeStruct((bsz, nc * L, DS), BF16),
        grid=(nc,),
        in_specs=[
            pl.BlockSpec((bsz, L, DS), lambda c: (0, c, z_blk)),
            pl.BlockSpec((bsz, L, DS), lambda c: (0, c, z_blk + 1)),
            pl.BlockSpec((bsz, L, GN2), lambda c: (0, c, bc_blk)),
            pl.BlockSpec((bsz, L, LANES), lambda c: (0, c, 0)),
            pl.BlockSpec(conv_w.shape, const2),
            pl.BlockSpec(conv_b.shape, const2),
            pl.BlockSpec(dt_bias.shape, const2),
            pl.BlockSpec(a_log.shape, const2),
            pl.BlockSpec(dskip_x.shape, const2),
            pl.BlockSpec(norm_w.shape, const2),
        ],
        out_specs=pl.BlockSpec((bsz, L, DS), lambda c: (0, c, 0)),
        scratch_shapes=[
            pltpu.VMEM((bsz, HALO, DS + GN2), F32),
            pltpu.VMEM((bsz, SSD_STATE, DS), F32),
        ],
        compiler_params=_params(("arbitrary",)),
        name="ssd",
    )(proj, proj, proj, dt_raw, conv_w, conv_b, dt_bias, a_log, dskip_x, norm_w)


def _first_lane_where(cond, lane):
    return jnp.min(jnp.where(cond, lane, LANES), axis=-1, keepdims=True)


def _route_rows(lg):
    G, E = MOE_GROUPS, EXPERTS_PER_GROUP
    lane = lax.broadcasted_iota(jnp.int32, lg.shape, 1)
    neg = -jnp.inf
    coarse = lane < G
    lc = jnp.where(coarse, lg, neg)
    mc = jnp.max(lc, axis=-1, keepdims=True)
    pg = 1.0 / jnp.sum(jnp.exp(lc - mc), axis=-1, keepdims=True)
    gidx = _first_lane_where(jnp.logical_and(coarse, lc == mc), lane)
    lo = G + gidx * E
    sel = jnp.where(jnp.logical_and(lane >= lo, lane < lo + E), lg, neg)
    m1 = jnp.max(sel, axis=-1, keepdims=True)
    i1 = _first_lane_where(sel == m1, lane)
    sel2 = jnp.where(lane == i1, neg, sel)
    m2 = jnp.max(sel2, axis=-1, keepdims=True)
    i2 = _first_lane_where(sel2 == m2, lane)
    r = jnp.exp(m2 - m1)
    w1 = pg / (1.0 + r)
    w2 = pg * (r / (1.0 + r))
    out = jnp.where(lane == 0, (i1 - G).astype(F32), 0.0)
    out = jnp.where(lane == 1, (i2 - G).astype(F32), out)
    out = jnp.where(lane == 2, w1, out)
    return jnp.where(lane == 3, w2, out)


def _out_ln1_kernel(ym_ref, ys_ref, x_ref, g0_ref, b0_ref, wm_ref, ws_ref, g1_ref, b1_ref,
                    wr_ref, br_ref, h1_ref, rt_ref):
    for s in range(x_ref.shape[0] // OUT_ROWS):
        rows = slice(s * OUT_ROWS, (s + 1) * OUT_ROWS)
        h0 = _layer_norm(x_ref[rows, :], g0_ref[...], b0_ref[...])
        mix = (jnp.dot(ym_ref[rows, :], wm_ref[...], preferred_element_type=F32)
               + jnp.dot(ys_ref[rows, :], ws_ref[...], preferred_element_type=F32))
        h1 = _layer_norm(DEEPNORM_ALPHA * h0 + mix, g1_ref[...], b1_ref[...])
        h1_ref[rows, :] = h1
        rt_ref[rows, :] = _route_rows(_dot3(h1, wr_ref[...]) + br_ref[...])


def _out_ln1(ym, ys, x2, g0, b0, w_o, g1, b1, w_r, b_r, tm):
    T, D = x2.shape
    DM = ym.shape[1]
    DS = ys.shape[1]
    assert DM % DS == 0
    row = lambda i: (i, 0)
    const = lambda i: (0, 0)
    return pl.pallas_call(
        _out_ln1_kernel,
        out_shape=(jax.ShapeDtypeStruct((T, D), F32), jax.ShapeDtypeStruct((T, LANES), F32)),
        grid=(T // tm,),
        in_specs=[
            pl.BlockSpec((tm, DM), row), pl.BlockSpec((tm, DS), row), pl.BlockSpec((tm, D), row),
            pl.BlockSpec((1, D), const), pl.BlockSpec((1, D), const),
            pl.BlockSpec((DM, D), const), pl.BlockSpec((DS, D), lambda i: (DM // DS, 0)),
            pl.BlockSpec((1, D), const), pl.BlockSpec((1, D), const),
            pl.BlockSpec((D, LANES), const), pl.BlockSpec((1, LANES), const),
        ],
        out_specs=(pl.BlockSpec((tm, D), row), pl.BlockSpec((tm, LANES), row)),
        compiler_params=_params(("parallel",)),
        name="out_ln1",
    )(ym, ys, x2, g0, b0, w_o, w_o, g1, b1, w_r, b_r)


def _moe_kernel(ie_ref, ir_ref, inb_ref, nit_ref, x_hbm, wg_ref, wu_ref, wd_ref, o_hbm,
                xbuf, acc, xsem, osem):
    i = pl.program_id(0)
    f = pl.program_id(1)
    nf = pl.num_programs(1)
    n_items = nit_ref[0]
    live = i < n_items
    slot = i % 2

    def x_copy(item, s, j):
        rows = pl.ds(pl.multiple_of(ir_ref[item] + j * MOE_BLOCK, MOE_BLOCK), MOE_BLOCK)
        return pltpu.make_async_copy(x_hbm.at[rows], xbuf.at[s, pl.ds(j * MOE_BLOCK, MOE_BLOCK)],
                                     xsem.at[s])

    def o_copy(item, s, j):
        rows = pl.ds(pl.multiple_of(ir_ref[item] + j * MOE_BLOCK, MOE_BLOCK), MOE_BLOCK)
        return pltpu.make_async_copy(acc.at[s, pl.ds(j * MOE_BLOCK, MOE_BLOCK)], o_hbm.at[rows],
                                     osem.at[s])

    def for_blocks(item, fn):
        n = inb_ref[item]
        for j in range(MOE_ITEM_BLOCKS):
            pl.when(j < n)(functools.partial(fn, j))

    @pl.when(jnp.logical_and(live, f == 0))
    def _():
        @pl.when(i == 0)
        def _():
            for_blocks(i, lambda j: x_copy(i, slot, j).start())

        @pl.when(i + 1 < n_items)
        def _():
            for_blocks(i + 1, lambda j: x_copy(i + 1, 1 - slot, j).start())

        for_blocks(i, lambda j: x_copy(i, slot, j).wait())

        @pl.when(i >= 2)
        def _():
            for_blocks(i - 2, lambda j: o_copy(i - 2, slot, j).wait())

    @pl.when(live)
    def _():
        def compute(n_rows):
            rows = pl.ds(0, n_rows)
            xb = xbuf[slot, rows, :].astype(BF16)
            a = jnp.dot(xb, wg_ref[0].astype(BF16), preferred_element_type=F32)
            u = jnp.dot(xb, wu_ref[0].astype(BF16), preferred_element_type=F32)
            part = jnp.dot((_silu(a) * u).astype(BF16), wd_ref[0].astype(BF16),
                           preferred_element_type=F32)

            @pl.when(f == 0)
            def _():
                acc[slot, rows, :] = part

            @pl.when(f > 0)
            def _():
                acc[slot, rows, :] += part

        for n in range(1, MOE_ITEM_BLOCKS + 1):
            pl.when(inb_ref[i] == n)(functools.partial(compute, n * MOE_BLOCK))

    @pl.when(jnp.logical_and(live, f == nf - 1))
    def _():
        for_blocks(i, lambda j: o_copy(i, slot, j).start())

        @pl.when(i == n_items - 1)
        def _():
            for_blocks(i, lambda j: o_copy(i, slot, j).wait())

            @pl.when(i >= 1)
            def _():
                for_blocks(i - 1, lambda j: o_copy(i - 1, 1 - slot, j).wait())

            zrows = pl.ds(0, MOE_BLOCK)
            xbuf[slot, zrows, :] = jnp.zeros((MOE_BLOCK, xbuf.shape[2]), xbuf.dtype)

            def z_copy(b):
                rows = pl.ds(pl.multiple_of(b * MOE_BLOCK, MOE_BLOCK), MOE_BLOCK)
                return pltpu.make_async_copy(xbuf.at[slot, zrows], o_hbm.at[rows], osem.at[slot])

            first, last = nit_ref[1], o_hbm.shape[0] // MOE_BLOCK
            lax.fori_loop(first, last, lambda b, c: (z_copy(b).start(), c)[1], 0)
            lax.fori_loop(first, last, lambda b, c: (z_copy(b).wait(), c)[1], 0)


def _moe_experts(item_e, item_row0, item_nblk, n_items, x_sorted, w_gate, w_up, w_down):
    NR, D = x_sorted.shape
    NI = item_e.shape[0]
    FF = w_gate.shape[2]
    tf = MOE_FF_TILE
    nf = FF // tf
    cap = MOE_ITEM_BLOCKS * MOE_BLOCK

    def expert(i, ie, nit):
        return ie[jnp.minimum(i, nit[0] - 1)]

    def ff(i, f, nit):
        return jnp.where(i < nit[0], f, nf - 1)

    return pl.pallas_call(
        _moe_kernel,
        out_shape=jax.ShapeDtypeStruct((NR, D), F32),
        grid_spec=pltpu.PrefetchScalarGridSpec(
            num_scalar_prefetch=4,
            grid=(NI, nf),
            in_specs=[
                pl.BlockSpec(memory_space=pl.ANY),
                pl.BlockSpec((1, D, tf),
                             lambda i, f, ie, ir, inb, nit: (expert(i, ie, nit), 0, ff(i, f, nit))),
                pl.BlockSpec((1, D, tf),
                             lambda i, f, ie, ir, inb, nit: (expert(i, ie, nit), 0, ff(i, f, nit))),
                pl.BlockSpec((1, tf, D),
                             lambda i, f, ie, ir, inb, nit: (expert(i, ie, nit), ff(i, f, nit), 0)),
            ],
            out_specs=pl.BlockSpec(memory_space=pl.ANY),
            scratch_shapes=[
                pltpu.VMEM((2, cap, D), F32),
                pltpu.VMEM((2, cap, D), F32),
                pltpu.SemaphoreType.DMA((2,)),
                pltpu.SemaphoreType.DMA((2,)),
            ],
        ),
        compiler_params=_params(("arbitrary", "arbitrary")),
        name="moe_experts",
    )(item_e, item_row0, item_nblk, n_items, x_sorted, w_gate, w_up, w_down)


def _combine_ln2_kernel(h_ref, *refs):
    m_refs, (w_ref, g_ref, b_ref, o_ref) = refs[:TOP_K], refs[TOP_K:]
    w = w_ref[...]
    moe = m_refs[0][...] * w[:, 0:1]
    for k in range(1, TOP_K):
        moe = moe + m_refs[k][...] * w[:, k:k + 1]
    o_ref[...] = _layer_norm(DEEPNORM_ALPHA * h_ref[...] + moe, g_ref[...], b_ref[...])


def _combine_ln2(h1, picked, wts, g, b, tm):
    T, D = h1.shape
    row = lambda i: (i, 0)
    const = lambda i: (0, 0)
    kth = [pl.BlockSpec((tm, D), functools.partial(lambda k, i: (k * (T // tm) + i, 0), k))
           for k in range(TOP_K)]
    return pl.pallas_call(
        _combine_ln2_kernel,
        out_shape=jax.ShapeDtypeStruct((T, D), F32),
        grid=(T // tm,),
        in_specs=[pl.BlockSpec((tm, D), row), *kth, pl.BlockSpec((tm, TOP_K), row),
                  pl.BlockSpec((1, D), const), pl.BlockSpec((1, D), const)],
        out_specs=pl.BlockSpec((tm, D), row),
        compiler_params=_params(("parallel",)),
        name="combine_ln2",
    )(h1, *([picked] * TOP_K), wts, g, b)


def _block_diag_tiles(w):
    nb = w.shape[0]
    n_tiles = nb * QKV_BLOCK // MXU_DIM
    rows = w.reshape(n_tiles, MXU_DIM, QKV_BLOCK)
    tiled = jnp.tile(rows, (1, 1, MXU_DIM // QKV_BLOCK))
    r = jnp.arange(MXU_DIM)[:, None] // QKV_BLOCK
    c = jnp.arange(MXU_DIM)[None, :] // QKV_BLOCK
    return jnp.where(r == c, tiled, 0.0).astype(BF16)


def _pad_lanes(a, width=LANES):
    return jnp.pad(a, [(0, 0)] * (a.ndim - 1) + [(0, width - a.shape[-1])])


def _dispatch_kernel(e_ref, dest_ref, cnt_ref, rank_ref):
    R = e_ref.shape[0]
    sub = lax.broadcasted_iota(jnp.int32, (LANES, LANES), 0)
    lanes = lax.broadcasted_iota(jnp.int32, (LANES, LANES), 1)
    incl = (sub <= lanes).astype(BF16)

    def onehot_t(b):
        return sub == e_ref[b]

    def rank_pass(b, base):
        ot = onehot_t(b)
        otf = ot.astype(F32)
        seen = jnp.dot(ot.astype(BF16), incl, preferred_element_type=F32) + base
        rank_ref[b] = jnp.sum(jnp.where(ot, seen, 0.0), axis=0, keepdims=True) - 1.0
        return base + jnp.sum(otf, axis=1, keepdims=True)

    counts = lax.fori_loop(0, R, rank_pass, jnp.zeros((LANES, 1), F32), unroll=4)
    padded = jnp.ceil(counts * (1.0 / MOE_BLOCK)) * MOE_BLOCK
    strict = (lanes < sub).astype(BF16)
    start = _dot_exact_lhs(strict, jnp.broadcast_to(padded, (LANES, LANES)))[:, 0:1]

    def dest_pass(b, carry):
        dest_ref[b] = (jnp.sum(jnp.where(onehot_t(b), start, 0.0), axis=0, keepdims=True)
                       + rank_ref[b]).astype(jnp.int32)
        return carry

    lax.fori_loop(0, R, dest_pass, 0, unroll=4)
    cnt_ref[...] = jnp.broadcast_to(counts, (LANES, LANES)).T[0:8, :].astype(jnp.int32)


def _dispatch_dest(e_flat):
    TK = e_flat.shape[0]
    R = TK // LANES
    dest, counts = pl.pallas_call(
        _dispatch_kernel,
        out_shape=(jax.ShapeDtypeStruct((R, 1, LANES), jnp.int32),
                   jax.ShapeDtypeStruct((8, LANES), jnp.int32)),
        scratch_shapes=[pltpu.VMEM((R, 1, LANES), F32)],
        name="dispatch",
    )(e_flat.reshape(R, 1, LANES))
    return dest.reshape(TK), counts[0]


def _dispatch(e_flat, T, n_experts):
    TK = T * TOP_K
    NR = (TK // MOE_BLOCK + n_experts + 1) * MOE_BLOCK
    dest, counts = _dispatch_dest(e_flat)
    counts = counts[:n_experts]
    padded = (counts + MOE_BLOCK - 1) // MOE_BLOCK * MOE_BLOCK
    pad_end = jnp.cumsum(padded)
    pad_start = pad_end - padded
    tok_flat = jnp.tile(jnp.arange(T, dtype=jnp.int32), TOP_K)
    row_tok = (jnp.arange(NR, dtype=jnp.int32) % T).at[dest].set(tok_flat)
    cap = MOE_ITEM_BLOCKS * MOE_BLOCK
    n_it = (counts + cap - 1) // cap
    it_end = jnp.cumsum(n_it)
    it_start = it_end - n_it
    NI = TK // cap + n_experts
    idx = jnp.arange(NI, dtype=jnp.int32)
    item_e = jnp.minimum(jnp.searchsorted(it_end, idx, side='right'), n_experts - 1).astype(jnp.int32)
    k = idx - it_start[item_e]
    live = idx < it_end[-1]
    item_row0 = jnp.where(live, pad_start[item_e] + k * cap, 0).astype(jnp.int32)
    item_nblk = jnp.where(live, jnp.clip(padded[item_e] // MOE_BLOCK - k * MOE_ITEM_BLOCKS,
                                         0, MOE_ITEM_BLOCKS), 0).astype(jnp.int32)
    n_items = jnp.stack([it_end[-1], pad_end[-1] // MOE_BLOCK]).astype(jnp.int32)
    return dest, row_tok, item_e, item_row0, item_nblk, n_items


def kernel(x, ln_in_g, ln_in_b, w_in, conv_m_w, conv_m_b, w_q, w_k, w_v, w_if, b_if, mlstm_norm_w, mlstm_skip, conv_s_w, conv_s_b, dt_bias, a_log, d_skip, ssd_norm_w, w_out, ln1_g, ln1_b, w_router_coarse, b_router_coarse, w_router_fine, b_router_fine, w_gate_e, w_up_e, w_down_e, ln2_g, ln2_b):
    bsz, seq, D = x.shape
    T = bsz * seq
    nc = seq // CHUNK
    DM = conv_m_w.shape[-1]
    DS = ssd_norm_w.shape[-1]
    n_heads_s = DS // SSD_HEAD_DIM
    n_experts = w_gate_e.shape[1]
    assert w_in.shape[0] == DEPTH
    l = 0
    x2 = x.reshape(T, D)
    row = lambda a: a.reshape(1, -1)

    n_main = w_in.shape[-1] - n_heads_s
    w_dt = _pad_lanes(w_in[l, :, n_main:])
    w_in_t = jnp.swapaxes(w_in[l], 0, 1).astype(BF16)
    proj, dt_raw = _ln_proj(x2, row(ln_in_g), row(ln_in_b), w_in_t, n_main, w_dt, tm=1024, tn=1024)
    proj = proj.reshape(bsz, seq, n_main)
    dt_raw = dt_raw.reshape(bsz, seq, LANES)

    wif = _pad_lanes(w_if[l]).astype(BF16).reshape(3, DM, LANES)
    y_m = _mlstm(proj, bsz, nc, conv_m_w[l], row(conv_m_b[l]),
                 _block_diag_tiles(w_q[l]), _block_diag_tiles(w_k[l]), _block_diag_tiles(w_v[l]),
                 wif, _pad_lanes(row(b_if[l])), row(mlstm_norm_w[l]), row(mlstm_skip[l]))
    y_m = y_m.reshape(T, DM)

    y_s = _ssd(proj, dt_raw, bsz, nc, DM, conv_s_w[l], row(conv_s_b[l]),
               _pad_lanes(row(dt_bias[l])), _pad_lanes(row(a_log[l])),
               row(jnp.repeat(d_skip[l], SSD_HEAD_DIM)), row(ssd_norm_w[l]))
    y_s = y_s.reshape(T, DS)

    w_o = w_out[l].astype(BF16)
    w_r = _pad_lanes(jnp.concatenate([w_router_coarse[l], w_router_fine[l]], axis=1))
    b_r = _pad_lanes(row(jnp.concatenate([b_router_coarse[l], b_router_fine[l]])))
    h1, routed = _out_ln1(y_m, y_s, x2, row(ln_in_g), row(ln_in_b), w_o,
                          row(ln1_g[l]), row(ln1_b[l]), w_r, b_r, tm=512)

    e_flat = routed[:, :TOP_K].T.reshape(TOP_K * T).astype(jnp.int32)
    wts = routed[:, TOP_K:2 * TOP_K]
    dest, row_tok, item_e, item_row0, item_nblk, n_items = _dispatch(e_flat, T, n_experts)
    x_sorted = jnp.take(h1, row_tok, axis=0, mode='clip')
    yb = _moe_experts(item_e, item_row0, item_nblk, n_items, x_sorted,
                      w_gate_e[l], w_up_e[l], w_down_e[l])

    picked = jnp.take(yb, dest, axis=0, mode='clip')
    out = _combine_ln2(h1, picked, wts, row(ln2_g[l]), row(ln2_b[l]), tm=512)
    return out.reshape(bsz, seq, D)
```

```python
import functools
import math

import jax
import jax.numpy as jnp
from jax import lax
from jax.experimental import pallas as pl
from jax.experimental.pallas import tpu as pltpu

F32 = jnp.float32
BF16 = jnp.bfloat16

LANES = 128
MXU_DIM = 256
VMEM_LIMIT = 56 * 1024 * 1024

MLSTM_HEADS = 4
QKV_BLOCK = 4
CHUNK = 128
SSD_HEAD_DIM = 64
SSD_GROUPS = 4
SSD_STATE = 128
CONV_WIDTH = 4
MOE_GROUPS = 8
EXPERTS_PER_GROUP = 8
TOP_K = 2
MOE_BLOCK = 128
MOE_ITEM_BLOCKS = 4
MOE_FF_TILE = 512
DEPTH = 1
DEEPNORM_ALPHA = (2 * DEPTH) ** 0.25
LN_EPS = 1e-5
HALO = 8
LN_ROWS = 256
OUT_ROWS = 256


def _params(semantics):
    return pltpu.CompilerParams(dimension_semantics=semantics, vmem_limit_bytes=VMEM_LIMIT)


def _bdot(a, b):
    return jnp.dot(a.astype(BF16), b.astype(BF16), preferred_element_type=F32)


def _bdot_nt(a, b):
    return lax.dot_general(a.astype(BF16), b.astype(BF16), (((1,), (1,)), ((), ())),
                           preferred_element_type=F32)


def _split2(a):
    hi = a.astype(BF16)
    lo = (a - hi.astype(F32)).astype(BF16)
    return hi, lo


def _dot_exact_rhs(a, b01):
    hi, lo = _split2(a)
    return (jnp.dot(hi, b01, preferred_element_type=F32)
            + jnp.dot(lo, b01, preferred_element_type=F32))


def _dot_exact_lhs(a01, b):
    hi, lo = _split2(b)
    return (jnp.dot(a01, hi, preferred_element_type=F32)
            + jnp.dot(a01, lo, preferred_element_type=F32))


def _dot3(a, b):
    ah, al = _split2(a)
    bh, bl = _split2(b)
    return (jnp.dot(ah, bh, preferred_element_type=F32)
            + jnp.dot(al, bh, preferred_element_type=F32)
            + jnp.dot(ah, bl, preferred_element_type=F32))


def _sigmoid(x):
    return 1.0 / (1.0 + jnp.exp(-x))


def _silu(x):
    return x * _sigmoid(x)


def _softplus(x):
    return jnp.maximum(x, 0.0) + jnp.log1p(jnp.exp(-jnp.abs(x)))


def _layer_norm(x, g, b):
    mu = jnp.mean(x, axis=-1, keepdims=True)
    xc = x - mu
    var = jnp.mean(xc * xc, axis=-1, keepdims=True)
    return xc * lax.rsqrt(var + LN_EPS) * g + b


def _lower_tri(n, strict=False):
    r = lax.broadcasted_iota(jnp.int32, (n, n), 0)
    c = lax.broadcasted_iota(jnp.int32, (n, n), 1)
    return (r > c) if strict else (r >= c)


def _causal_conv(cur, halo_ref, w, b):
    L = cur.shape[0]
    ext = jnp.concatenate([halo_ref[...], cur], axis=0)
    halo_ref[...] = cur[L - HALO:, :]
    acc = b + w[CONV_WIDTH - 1:CONV_WIDTH, :] * cur
    for s in range(1, CONV_WIDTH):
        delayed = pltpu.roll(ext, s, axis=0)[HALO:, :]
        acc = acc + w[CONV_WIDTH - 1 - s:CONV_WIDTH - s, :] * delayed
    return acc


def _ln_proj_kernel(x_ref, g_ref, b_ref, w_ref, wdt_ref, o_ref, dt_ref, hn_ref):
    j = pl.program_id(1)

    @pl.when(j == 0)
    def _():
        def chunk(r, carry):
            rows = pl.ds(pl.multiple_of(r * LN_ROWS, LN_ROWS), LN_ROWS)
            hb = _layer_norm(x_ref[rows, :], g_ref[...], b_ref[...]).astype(BF16)
            hn_ref[rows, :] = hb
            dt_ref[rows, :] = _bdot_nt(hb, wdt_ref[...])
            return carry

        lax.fori_loop(0, x_ref.shape[0] // LN_ROWS, chunk, 0)

    o_ref[...] = _bdot_nt(hn_ref[...], w_ref[...])


def _ln_proj(x2, g, b, w_in_t, n_main, w_dt_t, tm, tn):
    T, D = x2.shape
    N = n_main
    return pl.pallas_call(
        _ln_proj_kernel,
        out_shape=(jax.ShapeDtypeStruct((T, N), F32), jax.ShapeDtypeStruct((T, LANES), F32)),
        grid=(T // tm, N // tn),
        in_specs=[
            pl.BlockSpec((tm, D), lambda i, j: (i, 0)),
            pl.BlockSpec((1, D), lambda i, j: (0, 0)),
            pl.BlockSpec((1, D), lambda i, j: (0, 0)),
            pl.BlockSpec((tn, D), lambda i, j: (j, 0)),
            pl.BlockSpec((LANES, D), lambda i, j: (0, 0)),
        ],
        out_specs=(pl.BlockSpec((tm, tn), lambda i, j: (i, j)),
                   pl.BlockSpec((tm, LANES), lambda i, j: (i, 0))),
        scratch_shapes=[pltpu.VMEM((tm, D), BF16)],
        compiler_params=_params(("parallel", "arbitrary")),
        name="ln_proj",
    )(x2, g, b, w_in_t, w_dt_t)


def _mlstm_kernel(xm_ref, og_ref, cw_ref, cb_ref, wq_ref, wk_ref, wv_ref, wif_ref, bif_ref,
                  nw_ref, skip_ref, y_ref, halo_ref, ct_ref, n_ref, m_ref):
    @pl.when(pl.program_id(0) == 0)
    def _():
        halo_ref[...] = jnp.zeros_like(halo_ref)
        ct_ref[...] = jnp.zeros_like(ct_ref)
        n_ref[...] = jnp.zeros_like(n_ref)
        m_ref[...] = jnp.zeros_like(m_ref)

    for b in range(xm_ref.shape[0]):
        _mlstm_chunk(xm_ref.at[b], og_ref.at[b], cw_ref, cb_ref, wq_ref, wk_ref, wv_ref, wif_ref,
                     bif_ref, nw_ref, skip_ref, y_ref.at[b], halo_ref.at[b],
                     ct_ref.at[b], n_ref.at[b], m_ref.at[b])


def _mlstm_chunk(xm_ref, og_ref, cw_ref, cb_ref, wq_ref, wk_ref, wv_ref, wif_ref, bif_ref,
                 nw_ref, skip_ref, y_ref, halo_ref, ct_ref, n_ref, m_ref):
    L, DM = xm_ref.shape
    H = MLSTM_HEADS
    Dh = DM // H
    xm = xm_ref[...]
    xc = _silu(_causal_conv(xm, halo_ref, cw_ref[...], cb_ref[...]))
    xcb = xc.astype(BF16)
    xmb = xm.astype(BF16)
    nblk = DM // MXU_DIM
    q = jnp.concatenate([jnp.dot(xcb[:, i * MXU_DIM:(i + 1) * MXU_DIM], wq_ref[i],
                                 preferred_element_type=F32) for i in range(nblk)], axis=1)
    k = jnp.concatenate([jnp.dot(xcb[:, i * MXU_DIM:(i + 1) * MXU_DIM], wk_ref[i],
                                 preferred_element_type=F32) for i in range(nblk)], axis=1)
    v = jnp.concatenate([jnp.dot(xmb[:, i * MXU_DIM:(i + 1) * MXU_DIM], wv_ref[i],
                                 preferred_element_type=F32) for i in range(nblk)], axis=1)
    qb = q.astype(BF16)
    vb = v.astype(BF16)
    gates = (jnp.dot(qb, wif_ref[0], preferred_element_type=F32)
             + jnp.dot(k.astype(BF16), wif_ref[1], preferred_element_type=F32)
             + jnp.dot(vb, wif_ref[2], preferred_element_type=F32)) + bif_ref[...]
    logf = jnp.minimum(gates, 0.0) - jnp.log1p(jnp.exp(-jnp.abs(gates)))
    tri = _lower_tri(L).astype(BF16)
    bcum = _dot_exact_lhs(tri, logf)
    gates_t = gates.T
    bcum_t = bcum.T
    causal = _lower_tri(L)
    ks = k * (1.0 / math.sqrt(Dh))

    outs = []
    for h in range(H):
        sl = slice(h * Dh, (h + 1) * Dh)
        qh = qb[:, sl]
        kh = ks[:, sl]
        vh = vb[:, sl]
        bc = bcum[:, H + h:H + h + 1]
        br = bcum_t[H + h:H + h + 1, :]
        ic = gates[:, h:h + 1]
        ir = gates_t[h:h + 1, :]
        m_prev = m_ref[h:h + 1, 0:1]
        dlog = jnp.where(causal, bc - br + ir, -jnp.inf)
        inter = bc + m_prev
        mt = jnp.maximum(inter, jnp.max(dlog, axis=-1, keepdims=True))
        w_intra = jnp.exp(dlog - mt)
        w_inter = jnp.exp(inter - mt)
        sc = _bdot_nt(qh, kh) * w_intra
        ct = ct_ref[h]
        num = _bdot(sc, vh) + w_inter * _bdot(qh, ct)
        qn = jnp.sum(q[:, sl] * n_ref[h:h + 1, :], axis=-1, keepdims=True)
        den = jnp.sum(sc, axis=-1, keepdims=True) + w_inter * qn
        hb = num / jnp.maximum(jnp.abs(den), jnp.exp(-mt))
        mu = jnp.mean(hb, axis=-1, keepdims=True)
        hc = hb - mu
        var = jnp.mean(hc * hc, axis=-1, keepdims=True)
        outs.append(hc * lax.rsqrt(var + LN_EPS))
        bl = bc[L - 1:L, :]
        wlog = bl - bc + ic
        m_new = jnp.maximum(bl + m_prev, jnp.max(wlog, axis=0, keepdims=True))
        ws = jnp.exp(wlog - m_new)
        decay = jnp.exp(bl + m_prev - m_new)
        kw = kh * ws
        ct_ref[h] = decay * ct + lax.dot_general(
            kw.astype(BF16), vh, (((0,), (0,)), ((), ())), preferred_element_type=F32)
        n_ref[h:h + 1, :] = decay * n_ref[h:h + 1, :] + jnp.sum(kw, axis=0, keepdims=True)
        m_ref[h:h + 1, :] = jnp.broadcast_to(m_new, (1, m_ref.shape[1]))

    hn = jnp.concatenate(outs, axis=1) * nw_ref[...] + skip_ref[...] * xc
    y_ref[...] = (_sigmoid(og_ref[...]) * hn).astype(y_ref.dtype)


def _mlstm(proj, bsz, nc, conv_w, conv_b, wq_bd, wk_bd, wv_bd, wif, bif, norm_w, skip):
    L = CHUNK
    DM = conv_w.shape[1]
    Dh = DM // MLSTM_HEADS
    const2 = lambda c: (0, 0)
    const3 = lambda c: (0, 0, 0)
    return pl.pallas_call(
        _mlstm_kernel,
        out_shape=jax.ShapeDtypeStruct((bsz, nc * L, DM), BF16),
        grid=(nc,),
        in_specs=[
            pl.BlockSpec((bsz, L, DM), lambda c: (0, c, 0)),
            pl.BlockSpec((bsz, L, DM), lambda c: (0, c, 1)),
            pl.BlockSpec(conv_w.shape, const2),
            pl.BlockSpec(conv_b.shape, const2),
            pl.BlockSpec(wq_bd.shape, const3),
            pl.BlockSpec(wk_bd.shape, const3),
            pl.BlockSpec(wv_bd.shape, const3),
            pl.BlockSpec(wif.shape, const3),
            pl.BlockSpec(bif.shape, const2),
            pl.BlockSpec(norm_w.shape, const2),
            pl.BlockSpec(skip.shape, const2),
        ],
        out_specs=pl.BlockSpec((bsz, L, DM), lambda c: (0, c, 0)),
        scratch_shapes=[
            pltpu.VMEM((bsz, HALO, DM), F32),
            pltpu.VMEM((bsz, MLSTM_HEADS, Dh, Dh), F32),
            pltpu.VMEM((bsz, 8, Dh), F32),
            pltpu.VMEM((bsz, 8, LANES), F32),
        ],
        compiler_params=_params(("arbitrary",)),
        name="mlstm",
    )(proj, proj, conv_w, conv_b, wq_bd, wk_bd, wv_bd, wif, bif, norm_w, skip)


def _ssd_kernel(z_ref, xs_ref, bc_ref, dt_ref, cw_ref, cb_ref, dtb_ref, alog_ref, dskip_ref,
                nw_ref, y_ref, halo_ref, st_ref):
    @pl.when(pl.program_id(0) == 0)
    def _():
        halo_ref[...] = jnp.zeros_like(halo_ref)
        st_ref[...] = jnp.zeros_like(st_ref)

    for b in range(xs_ref.shape[0]):
        _ssd_chunk(z_ref.at[b], xs_ref.at[b], bc_ref.at[b], dt_ref.at[b], cw_ref, cb_ref, dtb_ref,
                   alog_ref, dskip_ref, nw_ref, y_ref.at[b], halo_ref.at[b],
                   st_ref.at[b])


def _ssd_chunk(z_ref, xs_ref, bc_ref, dt_ref, cw_ref, cb_ref, dtb_ref, alog_ref, dskip_ref,
               nw_ref, y_ref, halo_ref, st_ref):
    L, DS = xs_ref.shape
    G, N, P = SSD_GROUPS, SSD_STATE, SSD_HEAD_DIM
    GN = G * N
    DG = DS // G
    cur = jnp.concatenate([xs_ref[...], bc_ref[...]], axis=1)
    xbc = _silu(_causal_conv(cur, halo_ref, cw_ref[...], cb_ref[...]))
    xh = xbc[:, :DS]
    bm = xbc[:, DS:DS + GN]
    cm = xbc[:, DS + GN:]

    dt = _softplus(dt_ref[...] + dtb_ref[...])
    a = -jnp.exp(alog_ref[...])
    adt = dt * a
    tri = _lower_tri(L).astype(BF16)
    a_cs = _dot_exact_lhs(tri, adt)
    a_cs_t = a_cs.T
    ea = jnp.exp(a_cs)
    a_last = a_cs[L - 1:L, :]
    dstate = jnp.exp(a_last - a_cs)
    er = lax.broadcasted_iota(jnp.int32, (LANES, DS), 0)
    ec = lax.broadcasted_iota(jnp.int32, (LANES, DS), 1)
    expand = (ec // P == er).astype(BF16)
    stacked = jnp.concatenate([dt, ea, dstate], axis=0)
    ex = _dot_exact_rhs(stacked, expand)
    dt_x, ea_x, ds_x = ex[:L], ex[L:2 * L], ex[2 * L:]
    xd = xh * dt_x
    xdd = xd * ds_x
    xdb = xd.astype(BF16)
    causal = _lower_tri(L)
    lane = lax.broadcasted_iota(jnp.int32, (L, LANES), 1)
    lo_half = lane < P

    y_parts = []
    heads_per_group = DG // P
    for g in range(G):
        gs = slice(g * N, (g + 1) * N)
        cg = cm[:, gs].astype(BF16)
        bg = bm[:, gs].astype(BF16)
        cbg = _bdot_nt(cg, bg)
        cols = slice(g * DG, (g + 1) * DG)
        st = st_ref[:, cols]
        y_off = jnp.dot(cg, st.astype(BF16), preferred_element_type=F32) * ea_x[:, cols]
        yd = []
        for pr in range(heads_per_group // 2):
            h0 = g * heads_per_group + 2 * pr
            sc = []
            for h in (h0, h0 + 1):
                diff = a_cs[:, h:h + 1] - a_cs_t[h:h + 1, :]
                sc.append((cbg * jnp.exp(jnp.where(causal, diff, -jnp.inf))).astype(BF16))
            lhs = jnp.concatenate(sc, axis=1)
            xp = xdb[:, h0 * P:(h0 + 2) * P]
            zero = jnp.zeros_like(xp)
            rhs = jnp.concatenate([jnp.where(lo_half, xp, zero),
                                   jnp.where(lo_half, zero, xp)], axis=0)
            yd.append(jnp.dot(lhs, rhs, preferred_element_type=F32))
        y_parts.append(jnp.concatenate(yd, axis=1) + y_off)
        st_ref[:, cols] = ea_x[L - 1:L, cols] * st + lax.dot_general(
            bg, xdd[:, cols].astype(BF16), (((0,), (0,)), ((), ())), preferred_element_type=F32)

    y = jnp.concatenate(y_parts, axis=1) + dskip_ref[...] * xh
    y = y * _silu(z_ref[...])
    normed = []
    for g in range(G):
        yg = y[:, g * DG:(g + 1) * DG]
        normed.append(yg * lax.rsqrt(jnp.mean(yg * yg, axis=-1, keepdims=True) + LN_EPS))
    y_ref[...] = (jnp.concatenate(normed, axis=1) * nw_ref[...]).astype(y_ref.dtype)


def _ssd(proj, dt_raw, bsz, nc, d_mlstm, conv_w, conv_b, dt_bias, a_log, dskip_x, norm_w):
    L = CHUNK
    DS = norm_w.shape[1]
    GN2 = 2 * SSD_GROUPS * SSD_STATE
    const2 = lambda c: (0, 0)
    z_blk = 2 * d_mlstm // DS
    bc_blk = (2 * d_mlstm + 2 * DS) // GN2
    return pl.pallas_call(
        _ssd_kernel,
        out_shape=jax.ShapeDtypeStruct((bsz, nc * L, DS), BF16),
        grid=(nc,),
        in_specs=[
            pl.BlockSpec((bsz, L, DS), lambda c: (0, c, z_blk)),
            pl.BlockSpec((bsz, L, DS), lambda c: (0, c, z_blk + 1)),
            pl.BlockSpec((bsz, L, GN2), lambda c: (0, c, bc_blk)),
            pl.BlockSpec((bsz, L, LANES), lambda c: (0, c, 0)),
            pl.BlockSpec(conv_w.shape, const2),
            pl.BlockSpec(conv_b.shape, const2),
            pl.BlockSpec(dt_bias.shape, const2),
            pl.BlockSpec(a_log.shape, const2),
            pl.BlockSpec(dskip_x.shape, const2),
            pl.BlockSpec(norm_w.shape, const2),
        ],
        out_specs=pl.BlockSpec((bsz, L, DS), lambda c: (0, c, 0)),
        scratch_shapes=[
            pltpu.VMEM((bsz, HALO, DS + GN2), F32),
            pltpu.VMEM((bsz, SSD_STATE, DS), F32),
        ],
        compiler_params=_params(("arbitrary",)),
        name="ssd",
    )(proj, proj, proj, dt_raw, conv_w, conv_b, dt_bias, a_log, dskip_x, norm_w)


def _first_lane_where(cond, lane):
    return jnp.min(jnp.where(cond, lane, LANES), axis=-1, keepdims=True)


def _route_rows(lg):
    G, E = MOE_GROUPS, EXPERTS_PER_GROUP
    lane = lax.broadcasted_iota(jnp.int32, lg.shape, 1)
    neg = -jnp.inf
    coarse = lane < G
    lc = jnp.where(coarse, lg, neg)
    mc = jnp.max(lc, axis=-1, keepdims=True)
    pg = 1.0 / jnp.sum(jnp.exp(lc - mc), axis=-1, keepdims=True)
    gidx = _first_lane_where(jnp.logical_and(coarse, lc == mc), lane)
    lo = G + gidx * E
    sel = jnp.where(jnp.logical_and(lane >= lo, lane < lo + E), lg, neg)
    m1 = jnp.max(sel, axis=-1, keepdims=True)
    i1 = _first_lane_where(sel == m1, lane)
    sel2 = jnp.where(lane == i1, neg, sel)
    m2 = jnp.max(sel2, axis=-1, keepdims=True)
    i2 = _first_lane_where(sel2 == m2, lane)
    r = jnp.exp(m2 - m1)
    w1 = pg / (1.0 + r)
    w2 = pg * (r / (1.0 + r))
    out = jnp.where(lane == 0, (i1 - G).astype(F32), 0.0)
    out = jnp.where(lane == 1, (i2 - G).astype(F32), out)
    out = jnp.where(lane == 2, w1, out)
    return jnp.where(lane == 3, w2, out)


def _out_ln1_kernel(ym_ref, ys_ref, x_ref, g0_ref, b0_ref, wm_ref, ws_ref, g1_ref, b1_ref,
                    wr_ref, br_ref, h1_ref, rt_ref):
    for s in range(x_ref.shape[0] // OUT_ROWS):
        rows = slice(s * OUT_ROWS, (s + 1) * OUT_ROWS)
        h0 = _layer_norm(x_ref[rows, :], g0_ref[...], b0_ref[...])
        mix = (jnp.dot(ym_ref[rows, :], wm_ref[...], preferred_element_type=F32)
               + jnp.dot(ys_ref[rows, :], ws_ref[...], preferred_element_type=F32))
        h1 = _layer_norm(DEEPNORM_ALPHA * h0 + mix, g1_ref[...], b1_ref[...])
        h1_ref[rows, :] = h1
        rt_ref[rows, :] = _route_rows(_dot3(h1, wr_ref[...]) + br_ref[...])


def _out_ln1(ym, ys, x2, g0, b0, w_o, g1, b1, w_r, b_r, tm):
    T, D = x2.shape
    DM = ym.shape[1]
    DS = ys.shape[1]
    assert DM % DS == 0
    row = lambda i: (i, 0)
    const = lambda i: (0, 0)
    return pl.pallas_call(
        _out_ln1_kernel,
        out_shape=(jax.ShapeDtypeStruct((T, D), F32), jax.ShapeDtypeStruct((T, LANES), F32)),
        grid=(T // tm,),
        in_specs=[
            pl.BlockSpec((tm, DM), row), pl.BlockSpec((tm, DS), row), pl.BlockSpec((tm, D), row),
            pl.BlockSpec((1, D), const), pl.BlockSpec((1, D), const),
            pl.BlockSpec((DM, D), const), pl.BlockSpec((DS, D), lambda i: (DM // DS, 0)),
            pl.BlockSpec((1, D), const), pl.BlockSpec((1, D), const),
            pl.BlockSpec((D, LANES), const), pl.BlockSpec((1, LANES), const),
        ],
        out_specs=(pl.BlockSpec((tm, D), row), pl.BlockSpec((tm, LANES), row)),
        compiler_params=_params(("parallel",)),
        name="out_ln1",
    )(ym, ys, x2, g0, b0, w_o, w_o, g1, b1, w_r, b_r)


def _moe_kernel(ie_ref, ir_ref, inb_ref, nit_ref, x_hbm, wg_ref, wu_ref, wd_ref, o_hbm,
                xbuf, acc, xsem, osem):
    i = pl.program_id(0)
    f = pl.program_id(1)
    nf = pl.num_programs(1)
    n_items = nit_ref[0]
    live = i < n_items
    slot = i % 2

    def x_copy(item, s, j):
        rows = pl.ds(pl.multiple_of(ir_ref[item] + j * MOE_BLOCK, MOE_BLOCK), MOE_BLOCK)
        return pltpu.make_async_copy(x_hbm.at[rows], xbuf.at[s, pl.ds(j * MOE_BLOCK, MOE_BLOCK)],
                                     xsem.at[s])

    def o_copy(item, s, j):
        rows = pl.ds(pl.multiple_of(ir_ref[item] + j * MOE_BLOCK, MOE_BLOCK), MOE_BLOCK)
        return pltpu.make_async_copy(acc.at[s, pl.ds(j * MOE_BLOCK, MOE_BLOCK)], o_hbm.at[rows],
                                     osem.at[s])

    def for_blocks(item, fn):
        n = inb_ref[item]
        for j in range(MOE_ITEM_BLOCKS):
            pl.when(j < n)(functools.partial(fn, j))

    @pl.when(jnp.logical_and(live, f == 0))
    def _():
        @pl.when(i == 0)
        def _():
            for_blocks(i, lambda j: x_copy(i, slot, j).start())

        @pl.when(i + 1 < n_items)
        def _():
            for_blocks(i + 1, lambda j: x_copy(i + 1, 1 - slot, j).start())

        for_blocks(i, lambda j: x_copy(i, slot, j).wait())

        @pl.when(i >= 2)
        def _():
            for_blocks(i - 2, lambda j: o_copy(i - 2, slot, j).wait())

    @pl.when(live)
    def _():
        def compute(n_rows):
            rows = pl.ds(0, n_rows)
            xb = xbuf[slot, rows, :].astype(BF16)
            a = jnp.dot(xb, wg_ref[0].astype(BF16), preferred_element_type=F32)
            u = jnp.dot(xb, wu_ref[0].astype(BF16), preferred_element_type=F32)
            part = jnp.dot((_silu(a) * u).astype(BF16), wd_ref[0].astype(BF16),
                           preferred_element_type=F32)

            @pl.when(f == 0)
            def _():
                acc[slot, rows, :] = part

            @pl.when(f > 0)
            def _():
                acc[slot, rows, :] += part

        for n in range(1, MOE_ITEM_BLOCKS + 1):
            pl.when(inb_ref[i] == n)(functools.partial(compute, n * MOE_BLOCK))

    @pl.when(jnp.logical_and(live, f == nf - 1))
    def _():
        for_blocks(i, lambda j: o_copy(i, slot, j).start())

        @pl.when(i == n_items - 1)
        def _():
            for_blocks(i, lambda j: o_copy(i, slot, j).wait())

            @pl.when(i >= 1)
            def _():
                for_blocks(i - 1, lambda j: o_copy(i - 1, 1 - slot, j).wait())

            zrows = pl.ds(0, MOE_BLOCK)
            xbuf[slot, zrows, :] = jnp.zeros((MOE_BLOCK, xbuf.shape[2]), xbuf.dtype)

            def z_copy(b):
                rows = pl.ds(pl.multiple_of(b * MOE_BLOCK, MOE_BLOCK), MOE_BLOCK)
                return pltpu.make_async_copy(xbuf.at[slot, zrows], o_hbm.at[rows], osem.at[slot])

            first, last = nit_ref[1], o_hbm.shape[0] // MOE_BLOCK
            lax.fori_loop(first, last, lambda b, c: (z_copy(b).start(), c)[1], 0)
            lax.fori_loop(first, last, lambda b, c: (z_copy(b).wait(), c)[1], 0)


def _moe_experts(item_e, item_row0, item_nblk, n_items, x_sorted, w_gate, w_up, w_down):
    NR, D = x_sorted.shape
    NI = item_e.shape[0]
    FF = w_gate.shape[2]
    tf = MOE_FF_TILE
    nf = FF // tf
    cap = MOE_ITEM_BLOCKS * MOE_BLOCK

    def expert(i, ie, nit):
        return ie[jnp.minimum(i, nit[0] - 1)]

    def ff(i, f, nit):
        return jnp.where(i < nit[0], f, nf - 1)

    return pl.pallas_call(
        _moe_kernel,
        out_shape=jax.ShapeDtypeStruct((NR, D), F32),
        grid_spec=pltpu.PrefetchScalarGridSpec(
            num_scalar_prefetch=4,
            grid=(NI, nf),
            in_specs=[
                pl.BlockSpec(memory_space=pl.ANY),
                pl.BlockSpec((1, D, tf),
                             lambda i, f, ie, ir, inb, nit: (expert(i, ie, nit), 0, ff(i, f, nit))),
                pl.BlockSpec((1, D, tf),
                             lambda i, f, ie, ir, inb, nit: (expert(i, ie, nit), 0, ff(i, f, nit))),
                pl.BlockSpec((1, tf, D),
                             lambda i, f, ie, ir, inb, nit: (expert(i, ie, nit), ff(i, f, nit), 0)),
            ],
            out_specs=pl.BlockSpec(memory_space=pl.ANY),
            scratch_shapes=[
                pltpu.VMEM((2, cap, D), F32),
                pltpu.VMEM((2, cap, D), F32),
                pltpu.SemaphoreType.DMA((2,)),
                pltpu.SemaphoreType.DMA((2,)),
            ],
        ),
        compiler_params=_params(("arbitrary", "arbitrary")),
        name="moe_experts",
    )(item_e, item_row0, item_nblk, n_items, x_sorted, w_gate, w_up, w_down)


def _combine_ln2_kernel(h_ref, *refs):
    m_refs, (w_ref, g_ref, b_ref, o_ref) = refs[:TOP_K], refs[TOP_K:]
    w = w_ref[...]
    moe = m_refs[0][...] * w[:, 0:1]
    for k in range(1, TOP_K):
        moe = moe + m_refs[k][...] * w[:, k:k + 1]
    o_ref[...] = _layer_norm(DEEPNORM_ALPHA * h_ref[...] + moe, g_ref[...], b_ref[...])


def _combine_ln2(h1, picked, wts, g, b, tm):
    T, D = h1.shape
    row = lambda i: (i, 0)
    const = lambda i: (0, 0)
    kth = [pl.BlockSpec((tm, D), functools.partial(lambda k, i: (k * (T // tm) + i, 0), k))
           for k in range(TOP_K)]
    return pl.pallas_call(
        _combine_ln2_kernel,
        out_shape=jax.ShapeDtypeStruct((T, D), F32),
        grid=(T // tm,),
        in_specs=[pl.BlockSpec((tm, D), row), *kth, pl.BlockSpec((tm, TOP_K), row),
                  pl.BlockSpec((1, D), const), pl.BlockSpec((1, D), const)],
        out_specs=pl.BlockSpec((tm, D), row),
        compiler_params=_params(("parallel",)),
        name="combine_ln2",
    )(h1, *([picked] * TOP_K), wts, g, b)


def _block_diag_tiles(w):
    nb = w.shape[0]
    n_tiles = nb * QKV_BLOCK // MXU_DIM
    rows = w.reshape(n_tiles, MXU_DIM, QKV_BLOCK)
    tiled = jnp.tile(rows, (1, 1, MXU_DIM // QKV_BLOCK))
    r = jnp.arange(MXU_DIM)[:, None] // QKV_BLOCK
    c = jnp.arange(MXU_DIM)[None, :] // QKV_BLOCK
    return jnp.where(r == c, tiled, 0.0).astype(BF16)


def _pad_lanes(a, width=LANES):
    return jnp.pad(a, [(0, 0)] * (a.ndim - 1) + [(0, width - a.shape[-1])])


def _dispatch_kernel(e_ref, dest_ref, cnt_ref, rank_ref):
    R = e_ref.shape[0]
    sub = lax.broadcasted_iota(jnp.int32, (LANES, LANES), 0)
    lanes = lax.broadcasted_iota(jnp.int32, (LANES, LANES), 1)
    incl = (sub <= lanes).astype(BF16)

    def onehot_t(b):
        return sub == e_ref[b]

    def rank_pass(b, base):
        ot = onehot_t(b)
        otf = ot.astype(F32)
        seen = jnp.dot(ot.astype(BF16), incl, preferred_element_type=F32) + base
        rank_ref[b] = jnp.sum(jnp.where(ot, seen, 0.0), axis=0, keepdims=True) - 1.0
        return base + jnp.sum(otf, axis=1, keepdims=True)

    counts = lax.fori_loop(0, R, rank_pass, jnp.zeros((LANES, 1), F32), unroll=4)
    padded = jnp.ceil(counts * (1.0 / MOE_BLOCK)) * MOE_BLOCK
    strict = (lanes < sub).astype(BF16)
    start = _dot_exact_lhs(strict, jnp.broadcast_to(padded, (LANES, LANES)))[:, 0:1]

    def dest_pass(b, carry):
        dest_ref[b] = (jnp.sum(jnp.where(onehot_t(b), start, 0.0), axis=0, keepdims=True)
                       + rank_ref[b]).astype(jnp.int32)
        return carry

    lax.fori_loop(0, R, dest_pass, 0, unroll=4)
    cnt_ref[...] = jnp.broadcast_to(counts, (LANES, LANES)).T[0:8, :].astype(jnp.int32)


def _dispatch_dest(e_flat):
    TK = e_flat.shape[0]
    R = TK // LANES
    dest, counts = pl.pallas_call(
        _dispatch_kernel,
        out_shape=(jax.ShapeDtypeStruct((R, 1, LANES), jnp.int32),
                   jax.ShapeDtypeStruct((8, LANES), jnp.int32)),
        scratch_shapes=[pltpu.VMEM((R, 1, LANES), F32)],
        name="dispatch",
    )(e_flat.reshape(R, 1, LANES))
    return dest.reshape(TK), counts[0]


def _dispatch(e_flat, T, n_experts):
    TK = T * TOP_K
    NR = (TK // MOE_BLOCK + n_experts + 1) * MOE_BLOCK
    dest, counts = _dispatch_dest(e_flat)
    counts = counts[:n_experts]
    upto = jnp.arange(n_experts)[None, :] <= jnp.arange(n_experts)[:, None]
    prefix = lambda v: jnp.sum(jnp.where(upto, v[None, :], 0), axis=1)
    padded = (counts + MOE_BLOCK - 1) // MOE_BLOCK * MOE_BLOCK
    pad_end = prefix(padded)
    pad_start = pad_end - padded
    tok_flat = jnp.tile(jnp.arange(T, dtype=jnp.int32), TOP_K)
    row_tok = (jnp.arange(NR, dtype=jnp.int32) % T).at[dest].set(tok_flat)
    cap = MOE_ITEM_BLOCKS * MOE_BLOCK
    n_it = (counts + cap - 1) // cap
    it_end = prefix(n_it)
    it_start = it_end - n_it
    NI = TK // cap + n_experts
    idx = jnp.arange(NI, dtype=jnp.int32)
    item_e = jnp.minimum(jnp.sum(it_end[None, :] <= idx[:, None], axis=1),
                         n_experts - 1).astype(jnp.int32)
    k = idx - it_start[item_e]
    live = idx < it_end[-1]
    item_row0 = jnp.where(live, pad_start[item_e] + k * cap, 0).astype(jnp.int32)
    item_nblk = jnp.where(live, jnp.clip(padded[item_e] // MOE_BLOCK - k * MOE_ITEM_BLOCKS,
                                         0, MOE_ITEM_BLOCKS), 0).astype(jnp.int32)
    n_items = jnp.stack([it_end[-1], pad_end[-1] // MOE_BLOCK]).astype(jnp.int32)
    return dest, row_tok, item_e, item_row0, item_nblk, n_items


def kernel(x, ln_in_g, ln_in_b, w_in, conv_m_w, conv_m_b, w_q, w_k, w_v, w_if, b_if, mlstm_norm_w, mlstm_skip, conv_s_w, conv_s_b, dt_bias, a_log, d_skip, ssd_norm_w, w_out, ln1_g, ln1_b, w_router_coarse, b_router_coarse, w_router_fine, b_router_fine, w_gate_e, w_up_e, w_down_e, ln2_g, ln2_b):
    bsz, seq, D = x.shape
    T = bsz * seq
    nc = seq // CHUNK
    DM = conv_m_w.shape[-1]
    DS = ssd_norm_w.shape[-1]
    n_heads_s = DS // SSD_HEAD_DIM
    n_experts = w_gate_e.shape[1]
    assert w_in.shape[0] == DEPTH
    l = 0
    x2 = x.reshape(T, D)
    row = lambda a: a.reshape(1, -1)

    n_main = w_in.shape[-1] - n_heads_s
    w_in_t = jnp.swapaxes(w_in[l], 0, 1).astype(BF16)
    w_dt_t = jnp.pad(w_in_t[n_main:], ((0, LANES - n_heads_s), (0, 0)))
    proj, dt_raw = _ln_proj(x2, row(ln_in_g), row(ln_in_b), w_in_t, n_main, w_dt_t, tm=1024, tn=1024)
    proj = proj.reshape(bsz, seq, n_main)
    dt_raw = dt_raw.reshape(bsz, seq, LANES)

    wif = _pad_lanes(w_if[l]).astype(BF16).reshape(3, DM, LANES)
    y_m = _mlstm(proj, bsz, nc, conv_m_w[l], row(conv_m_b[l]),
                 _block_diag_tiles(w_q[l]), _block_diag_tiles(w_k[l]), _block_diag_tiles(w_v[l]),
                 wif, _pad_lanes(row(b_if[l])), row(mlstm_norm_w[l]), row(mlstm_skip[l]))
    y_m = y_m.reshape(T, DM)

    y_s = _ssd(proj, dt_raw, bsz, nc, DM, conv_s_w[l], row(conv_s_b[l]),
               _pad_lanes(row(dt_bias[l])), _pad_lanes(row(a_log[l])),
               row(jnp.repeat(d_skip[l], SSD_HEAD_DIM)), row(ssd_norm_w[l]))
    y_s = y_s.reshape(T, DS)

    w_o = w_out[l].astype(BF16)
    w_r = _pad_lanes(jnp.concatenate([w_router_coarse[l], w_router_fine[l]], axis=1))
    b_r = _pad_lanes(row(jnp.concatenate([b_router_coarse[l], b_router_fine[l]])))
    h1, routed = _out_ln1(y_m, y_s, x2, row(ln_in_g), row(ln_in_b), w_o,
                          row(ln1_g[l]), row(ln1_b[l]), w_r, b_r, tm=512)

    e_flat = routed[:, :TOP_K].T.reshape(TOP_K * T).astype(jnp.int32)
    wts = routed[:, TOP_K:2 * TOP_K]
    dest, row_tok, item_e, item_row0, item_nblk, n_items = _dispatch(e_flat, T, n_experts)
    x_sorted = jnp.take(h1, row_tok, axis=0, mode='clip')
    yb = _moe_experts(item_e, item_row0, item_nblk, n_items, x_sorted,
                      w_gate_e[l], w_up_e[l], w_down_e[l])

    picked = jnp.take(yb, dest, axis=0, mode='clip')
    out = _combine_ln2(h1, picked, wts, row(ln2_g[l]), row(ln2_b[l]), tm=512)
    return out.reshape(bsz, seq, D)
```

```python
import functools
import math

import jax
import jax.numpy as jnp
from jax import lax
from jax.experimental import pallas as pl
from jax.experimental.pallas import tpu as pltpu

F32 = jnp.float32
BF16 = jnp.bfloat16

LANES = 128
MXU_DIM = 256
VMEM_LIMIT = 56 * 1024 * 1024

MLSTM_HEADS = 4
QKV_BLOCK = 4
CHUNK = 128
SSD_HEAD_DIM = 64
SSD_GROUPS = 4
SSD_STATE = 128
CONV_WIDTH = 4
MOE_GROUPS = 8
EXPERTS_PER_GROUP = 8
TOP_K = 2
MOE_BLOCK = 128
MOE_ITEM_BLOCKS = 4
MOE_FF_TILE = 512
DEPTH = 1
DEEPNORM_ALPHA = (2 * DEPTH) ** 0.25
LN_EPS = 1e-5
HALO = 8
LN_ROWS = 256
OUT_ROWS = 256


def _params(semantics):
    return pltpu.CompilerParams(dimension_semantics=semantics, vmem_limit_bytes=VMEM_LIMIT)


def _bdot(a, b):
    return jnp.dot(a.astype(BF16), b.astype(BF16), preferred_element_type=F32)


def _bdot_nt(a, b):
    return lax.dot_general(a.astype(BF16), b.astype(BF16), (((1,), (1,)), ((), ())),
                           preferred_element_type=F32)


def _split2(a):
    hi = a.astype(BF16)
    lo = (a - hi.astype(F32)).astype(BF16)
    return hi, lo


def _dot_exact_rhs(a, b01):
    hi, lo = _split2(a)
    return (jnp.dot(hi, b01, preferred_element_type=F32)
            + jnp.dot(lo, b01, preferred_element_type=F32))


def _dot_exact_lhs(a01, b):
    hi, lo = _split2(b)
    return (jnp.dot(a01, hi, preferred_element_type=F32)
            + jnp.dot(a01, lo, preferred_element_type=F32))


def _dot3(a, b):
    ah, al = _split2(a)
    bh, bl = _split2(b)
    return (jnp.dot(ah, bh, preferred_element_type=F32)
            + jnp.dot(al, bh, preferred_element_type=F32)
            + jnp.dot(ah, bl, preferred_element_type=F32))


def _sigmoid(x):
    return jax.nn.sigmoid(x)


def _silu(x):
    return x * _sigmoid(x)


def _softplus(x):
    return jnp.maximum(x, 0.0) + jnp.log1p(jnp.exp(-jnp.abs(x)))


def _layer_norm(x, g, b):
    mu = jnp.mean(x, axis=-1, keepdims=True)
    xc = x - mu
    var = jnp.mean(xc * xc, axis=-1, keepdims=True)
    return xc * lax.rsqrt(var + LN_EPS) * g + b


def _lower_tri(n, strict=False):
    r = lax.broadcasted_iota(jnp.int32, (n, n), 0)
    c = lax.broadcasted_iota(jnp.int32, (n, n), 1)
    return (r > c) if strict else (r >= c)


def _causal_conv(cur, halo_ref, w, b):
    L = cur.shape[0]
    ext = jnp.concatenate([halo_ref[...], cur], axis=0)
    halo_ref[...] = cur[L - HALO:, :]
    acc = b + w[CONV_WIDTH - 1:CONV_WIDTH, :] * cur
    for s in range(1, CONV_WIDTH):
        delayed = pltpu.roll(ext, s, axis=0)[HALO:, :]
        acc = acc + w[CONV_WIDTH - 1 - s:CONV_WIDTH - s, :] * delayed
    return acc


def _ln_proj_kernel(x_ref, g_ref, b_ref, w_ref, wdt_ref, o_ref, dt_ref, hn_ref):
    j = pl.program_id(1)

    @pl.when(j == 0)
    def _():
        def chunk(r, carry):
            rows = pl.ds(pl.multiple_of(r * LN_ROWS, LN_ROWS), LN_ROWS)
            hb = _layer_norm(x_ref[rows, :], g_ref[...], b_ref[...]).astype(BF16)
            hn_ref[rows, :] = hb
            dt_ref[rows, :] = _bdot_nt(hb, wdt_ref[...])
            return carry

        lax.fori_loop(0, x_ref.shape[0] // LN_ROWS, chunk, 0)

    o_ref[...] = _bdot_nt(hn_ref[...], w_ref[...])


def _ln_proj(x2, g, b, w_in_t, n_main, w_dt_t, tm, tn):
    T, D = x2.shape
    N = n_main
    return pl.pallas_call(
        _ln_proj_kernel,
        out_shape=(jax.ShapeDtypeStruct((T, N), F32), jax.ShapeDtypeStruct((T, LANES), F32)),
        grid=(T // tm, N // tn),
        in_specs=[
            pl.BlockSpec((tm, D), lambda i, j: (i, 0)),
            pl.BlockSpec((1, D), lambda i, j: (0, 0)),
            pl.BlockSpec((1, D), lambda i, j: (0, 0)),
            pl.BlockSpec((tn, D), lambda i, j: (j, 0)),
            pl.BlockSpec((LANES, D), lambda i, j: (0, 0)),
        ],
        out_specs=(pl.BlockSpec((tm, tn), lambda i, j: (i, j)),
                   pl.BlockSpec((tm, LANES), lambda i, j: (i, 0))),
        scratch_shapes=[pltpu.VMEM((tm, D), BF16)],
        compiler_params=_params(("parallel", "arbitrary")),
        name="ln_proj",
    )(x2, g, b, w_in_t, w_dt_t)


def _mlstm_kernel(xm_ref, og_ref, cw_ref, cb_ref, wq_ref, wk_ref, wv_ref, wif_ref, bif_ref,
                  nw_ref, skip_ref, y_ref, halo_ref, ct_ref, n_ref, m_ref):
    @pl.when(pl.program_id(0) == 0)
    def _():
        halo_ref[...] = jnp.zeros_like(halo_ref)
        ct_ref[...] = jnp.zeros_like(ct_ref)
        n_ref[...] = jnp.zeros_like(n_ref)
        m_ref[...] = jnp.zeros_like(m_ref)

    for b in range(xm_ref.shape[0]):
        _mlstm_chunk(xm_ref.at[b], og_ref.at[b], cw_ref, cb_ref, wq_ref, wk_ref, wv_ref, wif_ref,
                     bif_ref, nw_ref, skip_ref, y_ref.at[b], halo_ref.at[b],
                     ct_ref.at[b], n_ref.at[b], m_ref.at[b])


def _mlstm_chunk(xm_ref, og_ref, cw_ref, cb_ref, wq_ref, wk_ref, wv_ref, wif_ref, bif_ref,
                 nw_ref, skip_ref, y_ref, halo_ref, ct_ref, n_ref, m_ref):
    L, DM = xm_ref.shape
    H = MLSTM_HEADS
    Dh = DM // H
    xm = xm_ref[...]
    xc = _silu(_causal_conv(xm, halo_ref, cw_ref[...], cb_ref[...]))
    xcb = xc.astype(BF16)
    xmb = xm.astype(BF16)
    nblk = DM // MXU_DIM
    q = jnp.concatenate([jnp.dot(xcb[:, i * MXU_DIM:(i + 1) * MXU_DIM], wq_ref[i],
                                 preferred_element_type=F32) for i in range(nblk)], axis=1)
    k = jnp.concatenate([jnp.dot(xcb[:, i * MXU_DIM:(i + 1) * MXU_DIM], wk_ref[i],
                                 preferred_element_type=F32) for i in range(nblk)], axis=1)
    v = jnp.concatenate([jnp.dot(xmb[:, i * MXU_DIM:(i + 1) * MXU_DIM], wv_ref[i],
                                 preferred_element_type=F32) for i in range(nblk)], axis=1)
    qb = q.astype(BF16)
    vb = v.astype(BF16)
    gates = (jnp.dot(qb, wif_ref[0], preferred_element_type=F32)
             + jnp.dot(k.astype(BF16), wif_ref[1], preferred_element_type=F32)
             + jnp.dot(vb, wif_ref[2], preferred_element_type=F32)) + bif_ref[...]
    logf = jnp.minimum(gates, 0.0) - jnp.log1p(jnp.exp(-jnp.abs(gates)))
    tri = _lower_tri(L).astype(BF16)
    bcum = _dot_exact_lhs(tri, logf)
    gates_t = gates.T
    bcum_t = bcum.T
    causal = _lower_tri(L)
    ks = k * (1.0 / math.sqrt(Dh))

    outs = []
    for h in range(H):
        sl = slice(h * Dh, (h + 1) * Dh)
        qh = qb[:, sl]
        kh = ks[:, sl]
        vh = vb[:, sl]
        bc = bcum[:, H + h:H + h + 1]
        br = bcum_t[H + h:H + h + 1, :]
        ic = gates[:, h:h + 1]
        ir = gates_t[h:h + 1, :]
        m_prev = m_ref[h:h + 1, 0:1]
        dlog = jnp.where(causal, bc - br + ir, -jnp.inf)
        inter = bc + m_prev
        mt = jnp.maximum(inter, jnp.max(dlog, axis=-1, keepdims=True))
        w_intra = jnp.exp(dlog - mt)
        w_inter = jnp.exp(inter - mt)
        sc = _bdot_nt(qh, kh) * w_intra
        ct = ct_ref[h]
        num = _bdot(sc, vh) + w_inter * _bdot(qh, ct)
        qn = jnp.sum(q[:, sl] * n_ref[h:h + 1, :], axis=-1, keepdims=True)
        den = jnp.sum(sc, axis=-1, keepdims=True) + w_inter * qn
        hb = num / jnp.maximum(jnp.abs(den), jnp.exp(-mt))
        mu = jnp.mean(hb, axis=-1, keepdims=True)
        hc = hb - mu
        var = jnp.mean(hc * hc, axis=-1, keepdims=True)
        outs.append(hc * lax.rsqrt(var + LN_EPS))
        bl = bc[L - 1:L, :]
        wlog = bl - bc + ic
        m_new = jnp.maximum(bl + m_prev, jnp.max(wlog, axis=0, keepdims=True))
        ws = jnp.exp(wlog - m_new)
        decay = jnp.exp(bl + m_prev - m_new)
        kw = kh * ws
        ct_ref[h] = decay * ct + lax.dot_general(
            kw.astype(BF16), vh, (((0,), (0,)), ((), ())), preferred_element_type=F32)
        n_ref[h:h + 1, :] = decay * n_ref[h:h + 1, :] + jnp.sum(kw, axis=0, keepdims=True)
        m_ref[h:h + 1, :] = jnp.broadcast_to(m_new, (1, m_ref.shape[1]))

    hn = jnp.concatenate(outs, axis=1) * nw_ref[...] + skip_ref[...] * xc
    y_ref[...] = (_sigmoid(og_ref[...]) * hn).astype(y_ref.dtype)


def _mlstm(proj, bsz, nc, conv_w, conv_b, wq_bd, wk_bd, wv_bd, wif, bif, norm_w, skip):
    L = CHUNK
    DM = conv_w.shape[1]
    Dh = DM // MLSTM_HEADS
    const2 = lambda c: (0, 0)
    const3 = lambda c: (0, 0, 0)
    return pl.pallas_call(
        _mlstm_kernel,
        out_shape=jax.ShapeDtypeStruct((bsz, nc * L, DM), BF16),
        grid=(nc,),
        in_specs=[
            pl.BlockSpec((bsz, L, DM), lambda c: (0, c, 0)),
            pl.BlockSpec((bsz, L, DM), lambda c: (0, c, 1)),
            pl.BlockSpec(conv_w.shape, const2),
            pl.BlockSpec(conv_b.shape, const2),
            pl.BlockSpec(wq_bd.shape, const3),
            pl.BlockSpec(wk_bd.shape, const3),
            pl.BlockSpec(wv_bd.shape, const3),
            pl.BlockSpec(wif.shape, const3),
            pl.BlockSpec(bif.shape, const2),
            pl.BlockSpec(norm_w.shape, const2),
            pl.BlockSpec(skip.shape, const2),
        ],
        out_specs=pl.BlockSpec((bsz, L, DM), lambda c: (0, c, 0)),
        scratch_shapes=[
            pltpu.VMEM((bsz, HALO, DM), F32),
            pltpu.VMEM((bsz, MLSTM_HEADS, Dh, Dh), F32),
            pltpu.VMEM((bsz, 8, Dh), F32),
            pltpu.VMEM((bsz, 8, LANES), F32),
        ],
        compiler_params=_params(("arbitrary",)),
        name="mlstm",
    )(proj, proj, conv_w, conv_b, wq_bd, wk_bd, wv_bd, wif, bif, norm_w, skip)


def _ssd_kernel(z_ref, xs_ref, bc_ref, dt_ref, cw_ref, cb_ref, dtb_ref, alog_ref, dskip_ref,
                nw_ref, y_ref, halo_ref, st_ref):
    @pl.when(pl.program_id(0) == 0)
    def _():
        halo_ref[...] = jnp.zeros_like(halo_ref)
        st_ref[...] = jnp.zeros_like(st_ref)

    for b in range(xs_ref.shape[0]):
        _ssd_chunk(z_ref.at[b], xs_ref.at[b], bc_ref.at[b], dt_ref.at[b], cw_ref, cb_ref, dtb_ref,
                   alog_ref, dskip_ref, nw_ref, y_ref.at[b], halo_ref.at[b],
                   st_ref.at[b])


def _ssd_chunk(z_ref, xs_ref, bc_ref, dt_ref, cw_ref, cb_ref, dtb_ref, alog_ref, dskip_ref,
               nw_ref, y_ref, halo_ref, st_ref):
    L, DS = xs_ref.shape
    G, N, P = SSD_GROUPS, SSD_STATE, SSD_HEAD_DIM
    GN = G * N
    DG = DS // G
    cur = jnp.concatenate([xs_ref[...], bc_ref[...]], axis=1)
    xbc = _silu(_causal_conv(cur, halo_ref, cw_ref[...], cb_ref[...]))
    xh = xbc[:, :DS]
    bm = xbc[:, DS:DS + GN]
    cm = xbc[:, DS + GN:]

    dt = _softplus(dt_ref[...] + dtb_ref[...])
    a = -jnp.exp(alog_ref[...])
    adt = dt * a
    tri = _lower_tri(L).astype(BF16)
    a_cs = _dot_exact_lhs(tri, adt)
    a_cs_t = a_cs.T
    ea = jnp.exp(a_cs)
    a_last = a_cs[L - 1:L, :]
    dstate = jnp.exp(a_last - a_cs)
    er = lax.broadcasted_iota(jnp.int32, (LANES, DS), 0)
    ec = lax.broadcasted_iota(jnp.int32, (LANES, DS), 1)
    expand = (ec // P == er).astype(BF16)
    stacked = jnp.concatenate([dt, ea, dstate], axis=0)
    ex = _dot_exact_rhs(stacked, expand)
    dt_x, ea_x, ds_x = ex[:L], ex[L:2 * L], ex[2 * L:]
    xd = xh * dt_x
    xdd = xd * ds_x
    xdb = xd.astype(BF16)
    causal = _lower_tri(L)
    lane = lax.broadcasted_iota(jnp.int32, (L, LANES), 1)
    lo_half = lane < P

    y_parts = []
    heads_per_group = DG // P
    for g in range(G):
        gs = slice(g * N, (g + 1) * N)
        cg = cm[:, gs].astype(BF16)
        bg = bm[:, gs].astype(BF16)
        cbg = _bdot_nt(cg, bg)
        cols = slice(g * DG, (g + 1) * DG)
        st = st_ref[:, cols]
        y_off = jnp.dot(cg, st.astype(BF16), preferred_element_type=F32) * ea_x[:, cols]
        yd = []
        for pr in range(heads_per_group // 2):
            h0 = g * heads_per_group + 2 * pr
            sc = []
            for h in (h0, h0 + 1):
                diff = a_cs[:, h:h + 1] - a_cs_t[h:h + 1, :]
                sc.append((cbg * jnp.exp(jnp.where(causal, diff, -jnp.inf))).astype(BF16))
            lhs = jnp.concatenate(sc, axis=1)
            xp = xdb[:, h0 * P:(h0 + 2) * P]
            zero = jnp.zeros_like(xp)
            rhs = jnp.concatenate([jnp.where(lo_half, xp, zero),
                                   jnp.where(lo_half, zero, xp)], axis=0)
            yd.append(jnp.dot(lhs, rhs, preferred_element_type=F32))
        y_parts.append(jnp.concatenate(yd, axis=1) + y_off)
        st_ref[:, cols] = ea_x[L - 1:L, cols] * st + lax.dot_general(
            bg, xdd[:, cols].astype(BF16), (((0,), (0,)), ((), ())), preferred_element_type=F32)

    y = jnp.concatenate(y_parts, axis=1) + dskip_ref[...] * xh
    y = y * _silu(z_ref[...])
    normed = []
    for g in range(G):
        yg = y[:, g * DG:(g + 1) * DG]
        normed.append(yg * lax.rsqrt(jnp.mean(yg * yg, axis=-1, keepdims=True) + LN_EPS))
    y_ref[...] = (jnp.concatenate(normed, axis=1) * nw_ref[...]).astype(y_ref.dtype)


def _ssd(proj, dt_raw, bsz, nc, d_mlstm, conv_w, conv_b, dt_bias, a_log, dskip_x, norm_w):
    L = CHUNK
    DS = norm_w.shape[1]
    GN2 = 2 * SSD_GROUPS * SSD_STATE
    const2 = lambda c: (0, 0)
    z_blk = 2 * d_mlstm // DS
    bc_blk = (2 * d_mlstm + 2 * DS) // GN2
    return pl.pallas_call(
        _ssd_kernel,
        out_shape=jax.ShapeDtypeStruct((bsz, nc * L, DS), BF16),
        grid=(nc,),
        in_specs=[
            pl.BlockSpec((bsz, L, DS), lambda c: (0, c, z_blk)),
            pl.BlockSpec((bsz, L, DS), lambda c: (0, c, z_blk + 1)),
            pl.BlockSpec((bsz, L, GN2), lambda c: (0, c, bc_blk)),
            pl.BlockSpec((bsz, L, LANES), lambda c: (0, c, 0)),
            pl.BlockSpec(conv_w.shape, const2),
            pl.BlockSpec(conv_b.shape, const2),
            pl.BlockSpec(dt_bias.shape, const2),
            pl.BlockSpec(a_log.shape, const2),
            pl.BlockSpec(dskip_x.shape, const2),
            pl.BlockSpec(norm_w.shape, const2),
        ],
        out_specs=pl.BlockSpec((bsz, L, DS), lambda c: (0, c, 0)),
        scratch_shapes=[
            pltpu.VMEM((bsz, HALO, DS + GN2), F32),
            pltpu.VMEM((bsz, SSD_STATE, DS), F32),
        ],
        compiler_params=_params(("arbitrary",)),
        name="ssd",
    )(proj, proj, proj, dt_raw, conv_w, conv_b, dt_bias, a_log, dskip_x, norm_w)


def _first_lane_where(cond, lane):
    return jnp.min(jnp.where(cond, lane, LANES), axis=-1, keepdims=True)


def _route_rows(lg):
    G, E = MOE_GROUPS, EXPERTS_PER_GROUP
    lane = lax.broadcasted_iota(jnp.int32, lg.shape, 1)
    neg = -jnp.inf
    coarse = lane < G
    lc = jnp.where(coarse, lg, neg)
    mc = jnp.max(lc, axis=-1, keepdims=True)
    pg = 1.0 / jnp.sum(jnp.exp(lc - mc), axis=-1, keepdims=True)
    gidx = _first_lane_where(jnp.logical_and(coarse, lc == mc), lane)
    lo = G + gidx * E
    sel = jnp.where(jnp.logical_and(lane >= lo, lane < lo + E), lg, neg)
    m1 = jnp.max(sel, axis=-1, keepdims=True)
    i1 = _first_lane_where(sel == m1, lane)
    sel2 = jnp.where(lane == i1, neg, sel)
    m2 = jnp.max(sel2, axis=-1, keepdims=True)
    i2 = _first_lane_where(sel2 == m2, lane)
    r = jnp.exp(m2 - m1)
    w1 = pg / (1.0 + r)
    w2 = pg * (r / (1.0 + r))
    out = jnp.where(lane == 0, (i1 - G).astype(F32), 0.0)
    out = jnp.where(lane == 1, (i2 - G).astype(F32), out)
    out = jnp.where(lane == 2, w1, out)
    return jnp.where(lane == 3, w2, out)


def _out_ln1_kernel(ym_ref, ys_ref, x_ref, g0_ref, b0_ref, wm_ref, ws_ref, g1_ref, b1_ref,
                    wr_ref, br_ref, h1_ref, rt_ref):
    for s in range(x_ref.shape[0] // OUT_ROWS):
        rows = slice(s * OUT_ROWS, (s + 1) * OUT_ROWS)
        h0 = _layer_norm(x_ref[rows, :], g0_ref[...], b0_ref[...])
        mix = (jnp.dot(ym_ref[rows, :], wm_ref[...], preferred_element_type=F32)
               + jnp.dot(ys_ref[rows, :], ws_ref[...], preferred_element_type=F32))
        h1 = _layer_norm(DEEPNORM_ALPHA * h0 + mix, g1_ref[...], b1_ref[...])
        h1_ref[rows, :] = h1
        rt_ref[rows, :] = _route_rows(_dot3(h1, wr_ref[...]) + br_ref[...])


def _out_ln1(ym, ys, x2, g0, b0, w_o, g1, b1, w_r, b_r, tm):
    T, D = x2.shape
    DM = ym.shape[1]
    DS = ys.shape[1]
    assert DM % DS == 0
    row = lambda i: (i, 0)
    const = lambda i: (0, 0)
    return pl.pallas_call(
        _out_ln1_kernel,
        out_shape=(jax.ShapeDtypeStruct((T, D), F32), jax.ShapeDtypeStruct((T, LANES), F32)),
        grid=(T // tm,),
        in_specs=[
            pl.BlockSpec((tm, DM), row), pl.BlockSpec((tm, DS), row), pl.BlockSpec((tm, D), row),
            pl.BlockSpec((1, D), const), pl.BlockSpec((1, D), const),
            pl.BlockSpec((DM, D), const), pl.BlockSpec((DS, D), lambda i: (DM // DS, 0)),
            pl.BlockSpec((1, D), const), pl.BlockSpec((1, D), const),
            pl.BlockSpec((D, LANES), const), pl.BlockSpec((1, LANES), const),
        ],
        out_specs=(pl.BlockSpec((tm, D), row), pl.BlockSpec((tm, LANES), row)),
        compiler_params=_params(("parallel",)),
        name="out_ln1",
    )(ym, ys, x2, g0, b0, w_o, w_o, g1, b1, w_r, b_r)


def _moe_kernel(ie_ref, ir_ref, inb_ref, nit_ref, x_hbm, wg_ref, wu_ref, wd_ref, o_hbm,
                xbuf, acc, xsem, osem):
    i = pl.program_id(0)
    f = pl.program_id(1)
    nf = pl.num_programs(1)
    n_items = nit_ref[0]
    live = i < n_items
    slot = i % 2

    def x_copy(item, s, j):
        rows = pl.ds(pl.multiple_of(ir_ref[item] + j * MOE_BLOCK, MOE_BLOCK), MOE_BLOCK)
        return pltpu.make_async_copy(x_hbm.at[rows], xbuf.at[s, pl.ds(j * MOE_BLOCK, MOE_BLOCK)],
                                     xsem.at[s])

    def o_copy(item, s, j):
        rows = pl.ds(pl.multiple_of(ir_ref[item] + j * MOE_BLOCK, MOE_BLOCK), MOE_BLOCK)
        return pltpu.make_async_copy(acc.at[s, pl.ds(j * MOE_BLOCK, MOE_BLOCK)], o_hbm.at[rows],
                                     osem.at[s])

    def for_blocks(item, fn):
        n = inb_ref[item]
        for j in range(MOE_ITEM_BLOCKS):
            pl.when(j < n)(functools.partial(fn, j))

    @pl.when(jnp.logical_and(live, f == 0))
    def _():
        @pl.when(i == 0)
        def _():
            for_blocks(i, lambda j: x_copy(i, slot, j).start())

        @pl.when(i + 1 < n_items)
        def _():
            for_blocks(i + 1, lambda j: x_copy(i + 1, 1 - slot, j).start())

        for_blocks(i, lambda j: x_copy(i, slot, j).wait())

        @pl.when(i >= 2)
        def _():
            for_blocks(i - 2, lambda j: o_copy(i - 2, slot, j).wait())

    @pl.when(live)
    def _():
        def compute(n_rows):
            rows = pl.ds(0, n_rows)
            xb = xbuf[slot, rows, :].astype(BF16)
            a = jnp.dot(xb, wg_ref[0].astype(BF16), preferred_element_type=F32)
            u = jnp.dot(xb, wu_ref[0].astype(BF16), preferred_element_type=F32)
            part = jnp.dot((_silu(a) * u).astype(BF16), wd_ref[0].astype(BF16),
                           preferred_element_type=F32)

            @pl.when(f == 0)
            def _():
                acc[slot, rows, :] = part

            @pl.when(f > 0)
            def _():
                acc[slot, rows, :] += part

        for n in range(1, MOE_ITEM_BLOCKS + 1):
            pl.when(inb_ref[i] == n)(functools.partial(compute, n * MOE_BLOCK))

    @pl.when(jnp.logical_and(live, f == nf - 1))
    def _():
        for_blocks(i, lambda j: o_copy(i, slot, j).start())

        @pl.when(i == n_items - 1)
        def _():
            for_blocks(i, lambda j: o_copy(i, slot, j).wait())

            @pl.when(i >= 1)
            def _():
                for_blocks(i - 1, lambda j: o_copy(i - 1, 1 - slot, j).wait())

            zrows = pl.ds(0, MOE_BLOCK)
            xbuf[slot, zrows, :] = jnp.zeros((MOE_BLOCK, xbuf.shape[2]), xbuf.dtype)

            def z_copy(b):
                rows = pl.ds(pl.multiple_of(b * MOE_BLOCK, MOE_BLOCK), MOE_BLOCK)
                return pltpu.make_async_copy(xbuf.at[slot, zrows], o_hbm.at[rows], osem.at[slot])

            first, last = nit_ref[1], o_hbm.shape[0] // MOE_BLOCK
            lax.fori_loop(first, last, lambda b, c: (z_copy(b).start(), c)[1], 0)
            lax.fori_loop(first, last, lambda b, c: (z_copy(b).wait(), c)[1], 0)


def _moe_experts(item_e, item_row0, item_nblk, n_items, x_sorted, w_gate, w_up, w_down):
    NR, D = x_sorted.shape
    NI = item_e.shape[0]
    FF = w_gate.shape[2]
    tf = MOE_FF_TILE
    nf = FF // tf
    cap = MOE_ITEM_BLOCKS * MOE_BLOCK

    def expert(i, ie, nit):
        return ie[jnp.minimum(i, nit[0] - 1)]

    def ff(i, f, nit):
        return jnp.where(i < nit[0], f, nf - 1)

    return pl.pallas_call(
        _moe_kernel,
        out_shape=jax.ShapeDtypeStruct((NR, D), F32),
        grid_spec=pltpu.PrefetchScalarGridSpec(
            num_scalar_prefetch=4,
            grid=(NI, nf),
            in_specs=[
                pl.BlockSpec(memory_space=pl.ANY),
                pl.BlockSpec((1, D, tf),
                             lambda i, f, ie, ir, inb, nit: (expert(i, ie, nit), 0, ff(i, f, nit))),
                pl.BlockSpec((1, D, tf),
                             lambda i, f, ie, ir, inb, nit: (expert(i, ie, nit), 0, ff(i, f, nit))),
                pl.BlockSpec((1, tf, D),
                             lambda i, f, ie, ir, inb, nit: (expert(i, ie, nit), ff(i, f, nit), 0)),
            ],
            out_specs=pl.BlockSpec(memory_space=pl.ANY),
            scratch_shapes=[
                pltpu.VMEM((2, cap, D), F32),
                pltpu.VMEM((2, cap, D), F32),
                pltpu.SemaphoreType.DMA((2,)),
                pltpu.SemaphoreType.DMA((2,)),
            ],
        ),
        compiler_params=_params(("arbitrary", "arbitrary")),
        name="moe_experts",
    )(item_e, item_row0, item_nblk, n_items, x_sorted, w_gate, w_up, w_down)


def _combine_ln2_kernel(h_ref, *refs):
    m_refs, (w_ref, g_ref, b_ref, o_ref) = refs[:TOP_K], refs[TOP_K:]
    w = w_ref[...]
    moe = m_refs[0][...] * w[:, 0:1]
    for k in range(1, TOP_K):
        moe = moe + m_refs[k][...] * w[:, k:k + 1]
    o_ref[...] = _layer_norm(DEEPNORM_ALPHA * h_ref[...] + moe, g_ref[...], b_ref[...])


def _combine_ln2(h1, picked, wts, g, b, tm):
    T, D = h1.shape
    row = lambda i: (i, 0)
    const = lambda i: (0, 0)
    kth = [pl.BlockSpec((tm, D), functools.partial(lambda k, i: (k * (T // tm) + i, 0), k))
           for k in range(TOP_K)]
    return pl.pallas_call(
        _combine_ln2_kernel,
        out_shape=jax.ShapeDtypeStruct((T, D), F32),
        grid=(T // tm,),
        in_specs=[pl.BlockSpec((tm, D), row), *kth, pl.BlockSpec((tm, TOP_K), row),
                  pl.BlockSpec((1, D), const), pl.BlockSpec((1, D), const)],
        out_specs=pl.BlockSpec((tm, D), row),
        compiler_params=_params(("parallel",)),
        name="combine_ln2",
    )(h1, *([picked] * TOP_K), wts, g, b)


def _block_diag_tiles(w):
    nb = w.shape[0]
    n_tiles = nb * QKV_BLOCK // MXU_DIM
    rows = w.reshape(n_tiles, MXU_DIM, QKV_BLOCK)
    tiled = jnp.tile(rows, (1, 1, MXU_DIM // QKV_BLOCK))
    r = jnp.arange(MXU_DIM)[:, None] // QKV_BLOCK
    c = jnp.arange(MXU_DIM)[None, :] // QKV_BLOCK
    return jnp.where(r == c, tiled, 0.0).astype(BF16)


def _pad_lanes(a, width=LANES):
    return jnp.pad(a, [(0, 0)] * (a.ndim - 1) + [(0, width - a.shape[-1])])


def _dispatch_kernel(e_ref, dest_ref, cnt_ref, rank_ref):
    R = e_ref.shape[0]
    sub = lax.broadcasted_iota(jnp.int32, (LANES, LANES), 0)
    lanes = lax.broadcasted_iota(jnp.int32, (LANES, LANES), 1)
    incl = (sub <= lanes).astype(BF16)

    def onehot_t(b):
        return sub == e_ref[b]

    def rank_pass(b, base):
        ot = onehot_t(b)
        otf = ot.astype(F32)
        seen = jnp.dot(ot.astype(BF16), incl, preferred_element_type=F32) + base
        rank_ref[b] = jnp.sum(jnp.where(ot, seen, 0.0), axis=0, keepdims=True) - 1.0
        return base + jnp.sum(otf, axis=1, keepdims=True)

    counts = lax.fori_loop(0, R, rank_pass, jnp.zeros((LANES, 1), F32), unroll=4)
    padded = jnp.ceil(counts * (1.0 / MOE_BLOCK)) * MOE_BLOCK
    strict = (lanes < sub).astype(BF16)
    start = _dot_exact_lhs(strict, jnp.broadcast_to(padded, (LANES, LANES)))[:, 0:1]

    def dest_pass(b, carry):
        dest_ref[b] = (jnp.sum(jnp.where(onehot_t(b), start, 0.0), axis=0, keepdims=True)
                       + rank_ref[b]).astype(jnp.int32)
        return carry

    lax.fori_loop(0, R, dest_pass, 0, unroll=4)
    cnt_ref[...] = jnp.broadcast_to(counts, (LANES, LANES)).T[0:8, :].astype(jnp.int32)


def _dispatch_dest(e_flat):
    TK = e_flat.shape[0]
    R = TK // LANES
    dest, counts = pl.pallas_call(
        _dispatch_kernel,
        out_shape=(jax.ShapeDtypeStruct((R, 1, LANES), jnp.int32),
                   jax.ShapeDtypeStruct((8, LANES), jnp.int32)),
        scratch_shapes=[pltpu.VMEM((R, 1, LANES), F32)],
        name="dispatch",
    )(e_flat.reshape(R, 1, LANES))
    return dest.reshape(TK), counts[0]


def _dispatch(e_flat, T, n_experts):
    TK = T * TOP_K
    NR = (TK // MOE_BLOCK + n_experts + 1) * MOE_BLOCK
    dest, counts = _dispatch_dest(e_flat)
    counts = counts[:n_experts]
    upto = jnp.arange(n_experts)[None, :] <= jnp.arange(n_experts)[:, None]
    prefix = lambda v: jnp.sum(jnp.where(upto, v[None, :], 0), axis=1)
    padded = (counts + MOE_BLOCK - 1) // MOE_BLOCK * MOE_BLOCK
    pad_end = prefix(padded)
    pad_start = pad_end - padded
    tok_flat = jnp.tile(jnp.arange(T, dtype=jnp.int32), TOP_K)
    row_tok = (jnp.arange(NR, dtype=jnp.int32) % T).at[dest].set(tok_flat)
    cap = MOE_ITEM_BLOCKS * MOE_BLOCK
    n_it = (counts + cap - 1) // cap
    it_end = prefix(n_it)
    it_start = it_end - n_it
    NI = TK // cap + n_experts
    idx = jnp.arange(NI, dtype=jnp.int32)
    item_e = jnp.minimum(jnp.sum(it_end[None, :] <= idx[:, None], axis=1),
                         n_experts - 1).astype(jnp.int32)
    k = idx - it_start[item_e]
    live = idx < it_end[-1]
    item_row0 = jnp.where(live, pad_start[item_e] + k * cap, 0).astype(jnp.int32)
    item_nblk = jnp.where(live, jnp.clip(padded[item_e] // MOE_BLOCK - k * MOE_ITEM_BLOCKS,
                                         0, MOE_ITEM_BLOCKS), 0).astype(jnp.int32)
    n_items = jnp.stack([it_end[-1], pad_end[-1] // MOE_BLOCK]).astype(jnp.int32)
    return dest, row_tok, item_e, item_row0, item_nblk, n_items


def kernel(x, ln_in_g, ln_in_b, w_in, conv_m_w, conv_m_b, w_q, w_k, w_v, w_if, b_if, mlstm_norm_w, mlstm_skip, conv_s_w, conv_s_b, dt_bias, a_log, d_skip, ssd_norm_w, w_out, ln1_g, ln1_b, w_router_coarse, b_router_coarse, w_router_fine, b_router_fine, w_gate_e, w_up_e, w_down_e, ln2_g, ln2_b):
    bsz, seq, D = x.shape
    T = bsz * seq
    nc = seq // CHUNK
    DM = conv_m_w.shape[-1]
    DS = ssd_norm_w.shape[-1]
    n_heads_s = DS // SSD_HEAD_DIM
    n_experts = w_gate_e.shape[1]
    assert w_in.shape[0] == DEPTH
    l = 0
    x2 = x.reshape(T, D)
    row = lambda a: a.reshape(1, -1)

    n_main = w_in.shape[-1] - n_heads_s
    w_in_t = jnp.swapaxes(w_in[l], 0, 1).astype(BF16)
    w_dt_t = jnp.pad(w_in_t[n_main:], ((0, LANES - n_heads_s), (0, 0)))
    proj, dt_raw = _ln_proj(x2, row(ln_in_g), row(ln_in_b), w_in_t, n_main, w_dt_t, tm=1024, tn=1024)
    proj = proj.reshape(bsz, seq, n_main)
    dt_raw = dt_raw.reshape(bsz, seq, LANES)

    wif = _pad_lanes(w_if[l]).astype(BF16).reshape(3, DM, LANES)
    y_m = _mlstm(proj, bsz, nc, conv_m_w[l], row(conv_m_b[l]),
                 _block_diag_tiles(w_q[l]), _block_diag_tiles(w_k[l]), _block_diag_tiles(w_v[l]),
                 wif, _pad_lanes(row(b_if[l])), row(mlstm_norm_w[l]), row(mlstm_skip[l]))
    y_m = y_m.reshape(T, DM)

    y_s = _ssd(proj, dt_raw, bsz, nc, DM, conv_s_w[l], row(conv_s_b[l]),
               _pad_lanes(row(dt_bias[l])), _pad_lanes(row(a_log[l])),
               row(jnp.repeat(d_skip[l], SSD_HEAD_DIM)), row(ssd_norm_w[l]))
    y_s = y_s.reshape(T, DS)

    w_o = w_out[l].astype(BF16)
    w_r = _pad_lanes(jnp.concatenate([w_router_coarse[l], w_router_fine[l]], axis=1))
    b_r = _pad_lanes(row(jnp.concatenate([b_router_coarse[l], b_router_fine[l]])))
    h1, routed = _out_ln1(y_m, y_s, x2, row(ln_in_g), row(ln_in_b), w_o,
                          row(ln1_g[l]), row(ln1_b[l]), w_r, b_r, tm=512)

    e_flat = routed[:, :TOP_K].T.reshape(TOP_K * T).astype(jnp.int32)
    wts = routed[:, TOP_K:2 * TOP_K]
    dest, row_tok, item_e, item_row0, item_nblk, n_items = _dispatch(e_flat, T, n_experts)
    x_sorted = jnp.take(h1, row_tok, axis=0, mode='clip')
    yb = _moe_experts(item_e, item_row0, item_nblk, n_items, x_sorted,
                      w_gate_e[l], w_up_e[l], w_down_e[l])

    picked = jnp.take(yb, dest, axis=0, mode='clip')
    out = _combine_ln2(h1, picked, wts, row(ln2_g[l]), row(ln2_b[l]), tm=512)
    return out.reshape(bsz, seq, D)
```
